```python
import jax, jax.numpy as jnp
from jax import lax
import numpy as np

D_MODEL = 2048
BATCH = 1
SEQ = 8192
DEPTH = 2

PLE_DIM = 256
N_MIXERS = 2
RET_HEADS = 8
RET_QK_DIM = D_MODEL // RET_HEADS
RET_V_DIM = 2 * D_MODEL // RET_HEADS
RET_PROJ = 2 * RET_HEADS * RET_QK_DIM + 2 * RET_HEADS * RET_V_DIM
RET_CHUNK = 128
ROPE_BASE = 10000.0
POOL_WIDTH = 2 * D_MODEL
POOL_WINDOWS = (2, 4, 8, 16)
POOL_GROUPS = len(POOL_WINDOWS)
POOL_GROUP_DIM = POOL_WIDTH // POOL_GROUPS
N_RET_LAYERS = (DEPTH + 1) // 2
N_POOL_LAYERS = DEPTH // 2
NORM_EPS = 1e-6
GN_EPS = 1e-5

kernel_name = "retnet_pool_interleaved_hybrid"


def rms_norm(x, gain):
    x32 = x.astype(jnp.float32)
    y = x32 * lax.rsqrt(jnp.mean(x32 * x32, axis=-1, keepdims=True) + NORM_EPS)
    return (y * gain.astype(jnp.float32)).astype(x.dtype)


def rotate_pairs(x, cos, sin):
    xr = x.reshape(x.shape[:-1] + (x.shape[-1] // 2, 2))
    x0, x1 = xr[..., 0], xr[..., 1]
    out = jnp.stack([x0 * cos - x1 * sin, x1 * cos + x0 * sin], axis=-1)
    return out.reshape(x.shape)


def retention_branch(xn, w_in, w_out):
    b, s, _ = xn.shape
    h, dk, dv, c = RET_HEADS, RET_QK_DIM, RET_V_DIM, RET_CHUNK
    n = s // c
    f32 = jnp.float32
    q, k, v, g = jnp.split(xn @ w_in, [h * dk, 2 * h * dk, 2 * h * dk + h * dv], axis=-1)
    q = q.reshape(b, s, h, dk).astype(f32)
    k = k.reshape(b, s, h, dk).astype(f32)
    v = v.reshape(b, s, h, dv).astype(f32)
    pos = jnp.arange(s, dtype=f32)
    freq = ROPE_BASE ** (-jnp.linspace(0.0, 1.0, dk // 2, dtype=f32))
    ang = pos[:, None] * freq[None, :]
    cos = jnp.cos(ang)[None, :, None, :]
    sin = jnp.sin(ang)[None, :, None, :]
    q = rotate_pairs(q, cos, sin)
    k = rotate_pairs(k, cos, sin) * (dk ** -0.5)
    log_gamma = jnp.log1p(-(2.0 ** (-5.0 - jnp.arange(h, dtype=f32))))
    idx = jnp.arange(c, dtype=f32)
    diff = idx[:, None] - idx[None, :]
    inner_decay = jnp.where(diff[None] >= 0,
                            jnp.exp(jnp.maximum(diff, 0.0)[None] * log_gamma[:, None, None]),
                            0.0)
    q_decay = jnp.exp((idx + 1.0)[:, None] * log_gamma[None, :])
    k_decay = jnp.exp((c - 1.0 - idx)[:, None] * log_gamma[None, :])
    chunk_decay = jnp.exp(c * log_gamma)
    qc = q.reshape(b, n, c, h, dk)
    kc = k.reshape(b, n, c, h, dk)
    vc = v.reshape(b, n, c, h, dv)
    scores = jnp.einsum('bnihd,bnjhd->bnhij', qc, kc) * inner_decay
    y_inner = jnp.einsum('bnhij,bnjhv->bnihv', scores, vc)

    def step(state, inp):
        q_n, k_n, v_n = inp
        y = jnp.einsum('bihd,bhdv->bihv', q_n, state) * q_decay[None, :, :, None]
        state = state * chunk_decay[None, :, None, None] + jnp.einsum(
            'bjhd,bjhv->bhdv', k_n * k_decay[None, :, :, None], v_n)
        return state, y

    state0 = jnp.zeros((b, h, dk, dv), f32)
    _, y_cross = lax.scan(step, state0, (jnp.moveaxis(qc, 1, 0), jnp.moveaxis(kc, 1, 0),
                                         jnp.moveaxis(vc, 1, 0)))
    y = (y_inner + jnp.moveaxis(y_cross, 0, 1)).reshape(b, s, h, dv)
    mu = jnp.mean(y, axis=-1, keepdims=True)
    var = jnp.mean(jnp.square(y - mu), axis=-1, keepdims=True)
    y = ((y - mu) * lax.rsqrt(var + GN_EPS)).reshape(b, s, h * dv).astype(xn.dtype)
    return (jax.nn.silu(g) * y) @ w_out


def pool_branch(xn, w_in, w_group, scale, w_out):
    b, s, _ = xn.shape
    u, g = jnp.split(xn @ w_in, 2, axis=-1)
    u32 = u.astype(jnp.float32)
    csum = jnp.cumsum(u32, axis=1)
    t = jnp.arange(s)
    outs = []
    for gi, w in enumerate(POOL_WINDOWS):
        lo, hi = gi * POOL_GROUP_DIM, (gi + 1) * POOL_GROUP_DIM
        c_g = csum[..., lo:hi]
        c_lag = jnp.pad(c_g, ((0, 0), (w, 0), (0, 0)))[:, :s]
        cnt = jnp.minimum(t + 1, w).astype(jnp.float32)[None, :, None]
        outs.append((c_g - c_lag) / cnt - u32[..., lo:hi])
    d = jnp.stack(outs, axis=2).astype(xn.dtype)
    mixed = jnp.einsum('bsgc,gce->bsge', d, w_group).reshape(b, s, POOL_WIDTH) * scale
    return (mixed * jax.nn.silu(g)) @ w_out


def setup_inputs(seed: int = 0) -> dict:
    key = jax.random.key(seed)
    ks = jax.random.split(key, 16)
    nrm = jax.random.normal
    f32 = jnp.float32
    x = nrm(ks[0], (BATCH, SEQ, D_MODEL), f32)
    p = nrm(ks[1], (DEPTH, BATCH, SEQ, PLE_DIM), f32)
    ret_norm = 1.0 + 0.02 * nrm(ks[2], (N_RET_LAYERS, D_MODEL), f32)
    ret_w_in = nrm(ks[3], (N_RET_LAYERS, D_MODEL, RET_PROJ), f32) * D_MODEL ** -0.5
    ret_w_out = nrm(ks[4], (N_RET_LAYERS, RET_HEADS * RET_V_DIM, D_MODEL), f32) * (RET_HEADS * RET_V_DIM) ** -0.5
    pool_norm = 1.0 + 0.02 * nrm(ks[5], (N_POOL_LAYERS, D_MODEL), f32)
    pool_w_in = nrm(ks[6], (N_POOL_LAYERS, D_MODEL, 2 * POOL_WIDTH), f32) * D_MODEL ** -0.5
    pool_w_group = nrm(ks[7], (N_POOL_LAYERS, POOL_GROUPS, POOL_GROUP_DIM, POOL_GROUP_DIM), f32) * POOL_GROUP_DIM ** -0.5
    pool_scale = 1.0 + 0.1 * nrm(ks[8], (N_POOL_LAYERS, POOL_WIDTH), f32)
    pool_w_out = nrm(ks[9], (N_POOL_LAYERS, POOL_WIDTH, D_MODEL), f32) * POOL_WIDTH ** -0.5
    ple_norm = 1.0 + 0.02 * nrm(ks[10], (DEPTH, D_MODEL), f32)
    ple_w_gate = nrm(ks[11], (DEPTH, D_MODEL, D_MODEL), f32) * D_MODEL ** -0.5
    ple_w_proj = nrm(ks[12], (DEPTH, PLE_DIM, D_MODEL), f32) * PLE_DIM ** -0.5
    final_norm = 1.0 + 0.02 * nrm(ks[13], (D_MODEL,), f32)
    return {"x": x, "p": p, "ret_norm": ret_norm, "ret_w_in": ret_w_in, "ret_w_out": ret_w_out,
            "pool_norm": pool_norm, "pool_w_in": pool_w_in, "pool_w_group": pool_w_group,
            "pool_scale": pool_scale, "pool_w_out": pool_w_out, "ple_norm": ple_norm,
            "ple_w_gate": ple_w_gate, "ple_w_proj": ple_w_proj, "final_norm": final_norm}


def reference(x, p, ret_norm, ret_w_in, ret_w_out, pool_norm, pool_w_in, pool_w_group,
              pool_scale, pool_w_out, ple_norm, ple_w_gate, ple_w_proj, final_norm):
    h = x
    for i in range(DEPTH):
        j = i // N_MIXERS
        if i % N_MIXERS == 0:
            h = h + retention_branch(rms_norm(h, ret_norm[j]), ret_w_in[j], ret_w_out[j])
        else:
            h = h + pool_branch(rms_norm(h, pool_norm[j]), pool_w_in[j], pool_w_group[j],
                                pool_scale[j], pool_w_out[j])
        gate = jax.nn.sigmoid(rms_norm(h, ple_norm[i]) @ ple_w_gate[i])
        h = h + gate * (p[i] @ ple_w_proj[i])
    return rms_norm(h, final_norm)
```

```python
import functools

import jax
import jax.numpy as jnp
from jax import lax
from jax.experimental import pallas as pl
from jax.experimental.pallas import tpu as pltpu

F32 = jnp.float32
BF16 = jnp.bfloat16

RET_HEADS = 8
ROPE_BASE = 10000.0
POOL_WINDOWS = (2, 4, 8, 16)
NORM_EPS = 1e-6
GN_EPS = 1e-5

LANES = 128
POOL_HALO = 16
RET_BLOCK = 256
VMEM_LIMIT = 56 * 1024 * 1024


def _cparams(n_axes):
    return pltpu.CompilerParams(
        dimension_semantics=("arbitrary",) * n_axes,
        vmem_limit_bytes=VMEM_LIMIT,
    )


def _rms(x, gain):
    ms = jnp.mean(x * x, axis=-1, keepdims=True)
    return x * lax.rsqrt(ms + NORM_EPS) * gain


def _silu(x):
    return x * (1.0 / (1.0 + jnp.exp(-x)))


def _rmsnorm_kernel(x_ref, g_ref, o_ref):
    o_ref[...] = _rms(x_ref[...], g_ref[...]).astype(o_ref.dtype)


def _rmsnorm(x, gain, tm=1024):
    s, d = x.shape
    return pl.pallas_call(
        _rmsnorm_kernel,
        grid=(s // tm,),
        in_specs=[pl.BlockSpec((tm, d), lambda i: (i, 0)),
                  pl.BlockSpec((1, d), lambda i: (0, 0))],
        out_specs=pl.BlockSpec((tm, d), lambda i: (i, 0)),
        out_shape=jax.ShapeDtypeStruct((s, d), BF16),
        compiler_params=_cparams(1),
        name="rmsnorm",
    )(x, gain.reshape(1, d))


def _ret_inproj_kernel(xn_ref, w_ref, cos_ref, sin_ref, o_ref, wbf_ref, *,
                       n_q_tiles, n_rot_tiles, k_scale):
    n = pl.program_id(0)
    i = pl.program_id(1)

    @pl.when(i == 0)
    def _():
        wbf_ref[...] = w_ref[...].astype(BF16)

    acc = jnp.dot(xn_ref[...], wbf_ref[...], preferred_element_type=F32)
    tm, tn = acc.shape

    @pl.when(n < n_rot_tiles)
    def _():
        scale = jnp.where(n < n_q_tiles, 1.0, k_scale).astype(F32)
        lane = lax.broadcasted_iota(jnp.int32, (tm, LANES), 1)
        even = (lane & 1) == 0
        head_w = cos_ref.shape[1]
        for c in range(tn // LANES):
            xs = acc[:, c * LANES:(c + 1) * LANES]
            t0 = (c * LANES) % head_w
            cs = cos_ref[:, t0:t0 + LANES]
            sn = sin_ref[:, t0:t0 + LANES]
            partner = jnp.where(even, pltpu.roll(xs, LANES - 1, 1), pltpu.roll(xs, 1, 1))
            o_ref[:, c * LANES:(c + 1) * LANES] = (
                (xs * cs + partner * sn) * scale).astype(o_ref.dtype)

    @pl.when(n >= n_rot_tiles)
    def _():
        o_ref[...] = acc.astype(o_ref.dtype)


def _ret_inproj(xn, w_in, cos_t, sin_t, qk_width, k_scale, tm=1024, tn=1024):
    s, d = xn.shape
    n_out = w_in.shape[1]
    head_w = cos_t.shape[1]
    n_rot_tiles = 2 * qk_width // tn
    kern = functools.partial(_ret_inproj_kernel, n_q_tiles=qk_width // tn,
                             n_rot_tiles=n_rot_tiles, k_scale=k_scale)
    tab_map = lambda n, i: (jnp.where(n < n_rot_tiles, i, 0), 0)
    return pl.pallas_call(
        kern,
        grid=(n_out // tn, s // tm),
        in_specs=[pl.BlockSpec((tm, d), lambda n, i: (i, 0)),
                  pl.BlockSpec((d, tn), lambda n, i: (0, n)),
                  pl.BlockSpec((tm, head_w), tab_map),
                  pl.BlockSpec((tm, head_w), tab_map)],
        out_specs=pl.BlockSpec((tm, tn), lambda n, i: (i, n)),
        out_shape=jax.ShapeDtypeStruct((s, n_out), BF16),
        scratch_shapes=[pltpu.VMEM((d, tn), BF16)],
        compiler_params=_cparams(2),
        name="ret_inproj",
    )(xn, w_in, cos_t, sin_t)


def _retention_kernel(lg_ref, q_ref, k_ref, v_ref, g_ref, o_ref, state_ref, dmat_ref, *,
                      cb, n_inner):
    h = pl.program_id(0)
    t = pl.program_id(1)
    lg = lg_ref[h]

    @pl.when(t == 0)
    def _():
        state_ref[...] = jnp.zeros_like(state_ref)
        r = lax.broadcasted_iota(jnp.int32, (cb, cb), 0)
        c = lax.broadcasted_iota(jnp.int32, (cb, cb), 1)
        diff = (r - c).astype(F32)
        dmat_ref[...] = jnp.where(diff >= 0, jnp.exp(jnp.maximum(diff, 0.0) * lg), 0.0)

    idx = lax.broadcasted_iota(jnp.int32, (cb, 1), 0).astype(F32)
    q_decay = jnp.exp((idx + 1.0) * lg)
    k_decay = jnp.exp((cb - 1.0 - idx) * lg)
    block_decay = jnp.exp(jnp.full((1, 1), cb, F32) * lg)

    def body(c, carry):
        r0 = pl.multiple_of(c * cb, cb)
        q = q_ref[pl.ds(r0, cb), :]
        k = k_ref[pl.ds(r0, cb), :]
        v = v_ref[pl.ds(r0, cb), :]
        scores = lax.dot_general(q, k, (((1,), (1,)), ((), ())), preferred_element_type=F32)
        p = (scores * dmat_ref[...]).astype(BF16)
        state = state_ref[...]
        y = jnp.dot(p, v, preferred_element_type=F32)
        y = y + jnp.dot(q, state.astype(BF16), preferred_element_type=F32) * q_decay
        kd_t = (k.astype(F32) * k_decay).T.astype(BF16)
        state_ref[...] = state * block_decay + jnp.dot(kd_t, v, preferred_element_type=F32)
        mu = jnp.mean(y, axis=-1, keepdims=True)
        yc = y - mu
        var = jnp.mean(yc * yc, axis=-1, keepdims=True)
        yn = yc * lax.rsqrt(var + GN_EPS)
        g = g_ref[pl.ds(r0, cb), :].astype(F32)
        o_ref[pl.ds(r0, cb), :] = (_silu(g) * yn).astype(o_ref.dtype)
        return carry

    lax.fori_loop(0, n_inner, body, 0)


def _retention(qkvg, log_gamma, heads, dk, dv, tb=2048, cb=RET_BLOCK):
    s = qkvg.shape[0]
    k_col0 = heads * dk // dk
    v_col0 = 2 * heads * dk // dv
    g_col0 = v_col0 + heads
    kern = functools.partial(_retention_kernel, cb=cb, n_inner=tb // cb)
    return pl.pallas_call(
        kern,
        grid=(heads, s // tb),
        in_specs=[pl.BlockSpec(memory_space=pltpu.SMEM),
                  pl.BlockSpec((tb, dk), lambda h, t: (t, h)),
                  pl.BlockSpec((tb, dk), lambda h, t: (t, k_col0 + h)),
                  pl.BlockSpec((tb, dv), lambda h, t: (t, v_col0 + h)),
                  pl.BlockSpec((tb, dv), lambda h, t: (t, g_col0 + h))],
        out_specs=pl.BlockSpec((tb, dv), lambda h, t: (t, h)),
        out_shape=jax.ShapeDtypeStruct((s, heads * dv), BF16),
        scratch_shapes=[pltpu.VMEM((dk, dv), F32), pltpu.VMEM((cb, cb), F32)],
        compiler_params=_cparams(2),
        name="retention",
    )(log_gamma, qkvg, qkvg, qkvg, qkvg)


def _outproj_ple_kernel(z_ref, h_ref, p_ref, wout_ref, pn_ref, wgate_ref, wproj_ref,
                        nn_ref, *out_refs, final):
    h1 = h_ref[...] + jnp.dot(z_ref[...], wout_ref[...], preferred_element_type=F32)
    xn = _rms(h1, pn_ref[...]).astype(BF16)
    logits = jnp.dot(xn, wgate_ref[...], preferred_element_type=F32)
    gate = 1.0 / (1.0 + jnp.exp(-logits))
    emb = jnp.dot(p_ref[...].astype(BF16), wproj_ref[...], preferred_element_type=F32)
    h2 = h1 + gate * emb
    nxt = _rms(h2, nn_ref[...])
    if final:
        out_refs[0][...] = nxt
    else:
        out_refs[0][...] = h2
        out_refs[1][...] = nxt.astype(BF16)


def _outproj_ple(z, h, p, w_out, ple_norm, w_gate, w_proj, next_norm, final, tm=256):
    s, d = h.shape
    kz = z.shape[1]
    pd = p.shape[1]
    row = lambda i: (i, 0)
    const = lambda i: (0, 0)
    resident = lambda shape: pl.BlockSpec(shape, const, pipeline_mode=pl.Buffered(1))
    if final:
        out_shape = (jax.ShapeDtypeStruct((s, d), F32),)
        out_specs = (pl.BlockSpec((tm, d), row),)
    else:
        out_shape = (jax.ShapeDtypeStruct((s, d), F32), jax.ShapeDtypeStruct((s, d), BF16))
        out_specs = (pl.BlockSpec((tm, d), row), pl.BlockSpec((tm, d), row))
    return pl.pallas_call(
        functools.partial(_outproj_ple_kernel, final=final),
        grid=(s // tm,),
        in_specs=[pl.BlockSpec((tm, kz), row),
                  pl.BlockSpec((tm, d), row),
                  pl.BlockSpec((tm, pd), row),
                  resident((kz, d)),
                  resident((1, d)),
                  resident((d, d)),
                  resident((pd, d)),
                  resident((1, d))],
        out_specs=out_specs,
        out_shape=out_shape,
        compiler_params=_cparams(1),
        name="outproj_ple_final" if final else "outproj_ple",
    )(z, h, p, w_out, ple_norm.reshape(1, d), w_gate, w_proj, next_norm.reshape(1, d))


def _pool_kernel(xn_ref, wu_ref, wg_ref, wgrp_ref, scale_ref, o_ref, ubuf_ref, dbuf_ref, *,
                 windows):
    grp = pl.program_id(0)
    i = pl.program_id(1)
    tm = xn_ref.shape[0]
    xn = xn_ref[...]
    u = jnp.dot(xn, wu_ref[...], preferred_element_type=F32)

    @pl.when(i == 0)
    def _():
        ubuf_ref[0:POOL_HALO, :] = jnp.zeros((POOL_HALO, ubuf_ref.shape[1]), F32)

    ubuf_ref[POOL_HALO:POOL_HALO + tm, :] = u
    tok = i * tm + lax.broadcasted_iota(jnp.int32, (tm, 1), 0)

    for gi, w in enumerate(windows):
        @pl.when(grp == gi)
        def _(w=w):
            acc = u
            for j in range(1, w):
                acc = acc + ubuf_ref[POOL_HALO - j:POOL_HALO - j + tm, :]
            inv_cnt = 1.0 / jnp.minimum(tok + 1, w).astype(F32)
            dbuf_ref[...] = (acc * inv_cnt - u).astype(BF16)

    ubuf_ref[0:POOL_HALO, :] = ubuf_ref[tm:tm + POOL_HALO, :]
    mixed = jnp.dot(dbuf_ref[...], wgrp_ref[0], preferred_element_type=F32) * scale_ref[...]
    gate = jnp.dot(xn, wg_ref[...], preferred_element_type=F32)
    o_ref[...] = (mixed * _silu(gate)).astype(o_ref.dtype)


def _pool_mixer(xn, w_in, w_group, scale, tm=512):
    s, d = xn.shape
    n_groups, gd, _ = w_group.shape
    width = n_groups * gd
    return pl.pallas_call(
        functools.partial(_pool_kernel, windows=POOL_WINDOWS),
        grid=(n_groups, s // tm),
        in_specs=[pl.BlockSpec((tm, d), lambda g, i: (i, 0)),
                  pl.BlockSpec((d, gd), lambda g, i: (0, g)),
                  pl.BlockSpec((d, gd), lambda g, i: (0, n_groups + g)),
                  pl.BlockSpec((1, gd, gd), lambda g, i: (g, 0, 0)),
                  pl.BlockSpec((1, gd), lambda g, i: (0, g))],
        out_specs=pl.BlockSpec((tm, gd), lambda g, i: (i, g)),
        out_shape=jax.ShapeDtypeStruct((s, width), BF16),
        scratch_shapes=[pltpu.VMEM((POOL_HALO + tm, gd), F32), pltpu.VMEM((tm, gd), BF16)],
        compiler_params=_cparams(2),
        name="pool_mixer",
    )(xn, w_in, w_in, w_group, scale.reshape(1, width))


def _rotary_tables(seq, dk):
    pos = jnp.arange(seq, dtype=F32)
    freq = ROPE_BASE ** (-jnp.linspace(0.0, 1.0, dk // 2, dtype=F32))
    ang = pos[:, None] * freq[None, :]
    cos, sin = jnp.cos(ang), jnp.sin(ang)
    cos_t = jnp.repeat(cos, 2, axis=1)
    sin_t = jnp.stack([-sin, sin], axis=-1).reshape(seq, dk)
    return cos_t, sin_t


def kernel(x, p, ret_norm, ret_w_in, ret_w_out, pool_norm, pool_w_in, pool_w_group, pool_scale,
           pool_w_out, ple_norm, ple_w_gate, ple_w_proj, final_norm):
    b, s, d = x.shape
    assert b == 1
    heads = RET_HEADS
    dk = d // heads
    dv = ret_w_out.shape[1] // heads
    assert ret_w_in.shape[2] == 2 * heads * dk + 2 * heads * dv

    h0 = x[0]
    cos_t, sin_t = _rotary_tables(s, dk)
    log_gamma = jnp.log1p(-(2.0 ** (-5.0 - jnp.arange(heads, dtype=F32))))

    xn0 = _rmsnorm(h0, ret_norm[0])
    qkvg = _ret_inproj(xn0, ret_w_in[0], cos_t, sin_t, heads * dk, dk ** -0.5)
    yg = _retention(qkvg, log_gamma, heads, dk, dv)
    h2, xn2 = _outproj_ple(yg, h0, p[0, 0], ret_w_out[0].astype(BF16), ple_norm[0],
                           ple_w_gate[0].astype(BF16), ple_w_proj[0].astype(BF16),
                           pool_norm[0], final=False)

    z = _pool_mixer(xn2, pool_w_in[0].astype(BF16), pool_w_group[0].astype(BF16), pool_scale[0])
    (out,) = _outproj_ple(z, h2, p[1, 0], pool_w_out[0].astype(BF16), ple_norm[1],
                          ple_w_gate[1].astype(BF16), ple_w_proj[1].astype(BF16),
                          final_norm, final=True)
    return out[None]
```

```python
import functools

import jax
import jax.numpy as jnp
from jax import lax
from jax.experimental import pallas as pl
from jax.experimental.pallas import tpu as pltpu

F32 = jnp.float32
BF16 = jnp.bfloat16

RET_HEADS = 8
ROPE_BASE = 10000.0
POOL_WINDOWS = (2, 4, 8, 16)
NORM_EPS = 1e-6
GN_EPS = 1e-5

LANES = 128
SUBLANES = 8
MXU_WIDTH = 256
POOL_HALO = 32
RET_BLOCK = 256
VMEM_LIMIT = 56 * 1024 * 1024


def _cparams(n_axes):
    return pltpu.CompilerParams(
        dimension_semantics=("arbitrary",) * n_axes,
        vmem_limit_bytes=VMEM_LIMIT,
    )


def _rms(x, gain):
    ms = jnp.mean(x * x, axis=-1, keepdims=True)
    return x * lax.rsqrt(ms + NORM_EPS) * gain


def _sigmoid(x):
    return 0.5 * jnp.tanh(0.5 * x) + 0.5


def _silu(x):
    return x * _sigmoid(x)


def _rmsnorm_kernel(x_ref, g_ref, o_ref):
    o_ref[...] = _rms(x_ref[...], g_ref[...]).astype(o_ref.dtype)


def _rmsnorm(x, gain, tm=1024):
    _, s, d = x.shape
    return pl.pallas_call(
        _rmsnorm_kernel,
        grid=(s // tm,),
        in_specs=[pl.BlockSpec((None, tm, d), lambda i: (0, i, 0)),
                  pl.BlockSpec((1, d), lambda i: (0, 0))],
        out_specs=pl.BlockSpec((tm, d), lambda i: (i, 0)),
        out_shape=jax.ShapeDtypeStruct((s, d), BF16),
        compiler_params=_cparams(1),
        name="rmsnorm",
    )(x, gain)


def _ret_inproj_kernel(xn_ref, w_ref, cr_ref, sr_ref, ct_ref, st_ref, o_ref,
                       wbf_ref, cs_ref, sn_ref, *, n_q_tiles, n_rot_tiles, n_plain_tiles, k_scale):
    n = pl.program_id(0)
    i = pl.program_id(1)
    tn = o_ref.shape[1]

    @pl.when(i == 0)
    def _():
        wbf_ref[...] = w_ref[...].astype(BF16)

    def run(epilogue):
        for c in range(tn // MXU_WIDTH):
            cols = slice(c * MXU_WIDTH, (c + 1) * MXU_WIDTH)
            acc = jnp.dot(xn_ref[...], wbf_ref[:, cols], preferred_element_type=F32)
            o_ref[:, cols] = epilogue(acc).astype(o_ref.dtype)

    @pl.when(n < n_rot_tiles)
    def _():
        scale = jnp.where(n < n_q_tiles, 1.0, k_scale).astype(F32)
        ct, st = ct_ref[...] * scale, st_ref[...] * scale
        lane_t = lax.broadcasted_iota(jnp.int32, ct.shape, 1)
        st_signed = jnp.where((lane_t & 1) == 0, -st, st)
        ct_signed = jnp.where((lane_t & 1) == 0, -ct, ct)
        cr, sr = cr_ref[...], sr_ref[...]
        cs_ref[...] = ct * cr - st * sr
        sn_ref[...] = st_signed * cr + ct_signed * sr
        lane = lax.broadcasted_iota(jnp.int32, (xn_ref.shape[0], LANES), 1)
        even = (lane & 1) == 0

        def rotate(acc):
            outs = []
            for hcol in range(0, MXU_WIDTH, LANES):
                xs = acc[:, hcol:hcol + LANES]
                partner = jnp.where(even, pltpu.roll(xs, LANES - 1, 1), pltpu.roll(xs, 1, 1))
                outs.append(xs * cs_ref[:, hcol:hcol + LANES] + partner * sn_ref[:, hcol:hcol + LANES])
            return jnp.concatenate(outs, axis=1)

        run(rotate)

    @pl.when((n >= n_rot_tiles) & (n < n_plain_tiles))
    def _():
        run(lambda acc: acc)

    @pl.when(n >= n_plain_tiles)
    def _():
        run(_silu)


def _ret_inproj(xn, w_in, tabs, qk_width, v_width, k_scale, tm, tn=1024):
    s, d = xn.shape
    n_out = w_in.shape[2]
    cr, sr, ct, st = tabs
    head_w = cr.shape[1]
    assert head_w == MXU_WIDTH and cr.shape[0] == tm
    n_rot_tiles = 2 * qk_width // tn
    kern = functools.partial(_ret_inproj_kernel, n_q_tiles=qk_width // tn, n_rot_tiles=n_rot_tiles,
                             n_plain_tiles=n_rot_tiles + v_width // tn, k_scale=k_scale)
    return pl.pallas_call(
        kern,
        grid=(n_out // tn, s // tm),
        in_specs=[pl.BlockSpec((tm, d), lambda n, i: (i, 0)),
                  pl.BlockSpec((None, d, tn), lambda n, i: (0, 0, n)),
                  pl.BlockSpec((tm, head_w), lambda n, i: (0, 0)),
                  pl.BlockSpec((tm, head_w), lambda n, i: (0, 0)),
                  pl.BlockSpec((None, 1, head_w), lambda n, i: (i, 0, 0)),
                  pl.BlockSpec((None, 1, head_w), lambda n, i: (i, 0, 0))],
        out_specs=pl.BlockSpec((tm, tn), lambda n, i: (i, n)),
        out_shape=jax.ShapeDtypeStruct((s, n_out), BF16),
        scratch_shapes=[pltpu.VMEM((d, tn), BF16),
                        pltpu.VMEM((tm, head_w), F32),
                        pltpu.VMEM((tm, head_w), F32)],
        compiler_params=_cparams(2),
        name="ret_inproj",
    )(xn, w_in, cr, sr, ct, st)


def _retention_kernel(lg_ref, q_ref, k_ref, v_ref, g_ref, o_ref, state_ref, dmat_ref, *,
                      cb, n_inner):
    h = pl.program_id(0)
    t = pl.program_id(1)
    lg = lg_ref[h]

    @pl.when(t == 0)
    def _():
        state_ref[...] = jnp.zeros_like(state_ref)
        r = lax.broadcasted_iota(jnp.int32, (cb, cb), 0)
        c = lax.broadcasted_iota(jnp.int32, (cb, cb), 1)
        diff = (r - c).astype(F32)
        dmat_ref[...] = jnp.where(diff >= 0, jnp.exp(jnp.maximum(diff, 0.0) * lg), 0.0)

    idx = lax.broadcasted_iota(jnp.int32, (cb, 1), 0).astype(F32)
    q_decay = jnp.exp((idx + 1.0) * lg)
    k_decay = jnp.exp((cb - 1.0 - idx) * lg)
    block_decay = jnp.exp(jnp.full((1, 1), cb, F32) * lg)

    def body(c, carry):
        r0 = pl.multiple_of(c * cb, cb)
        q = q_ref[pl.ds(r0, cb), :]
        k = k_ref[pl.ds(r0, cb), :]
        v = v_ref[pl.ds(r0, cb), :]
        scores = lax.dot_general(q, k, (((1,), (1,)), ((), ())), preferred_element_type=F32)
        p = (scores * dmat_ref[...]).astype(BF16)
        state = state_ref[...]
        y = jnp.dot(p, v, preferred_element_type=F32)
        y = y + jnp.dot(q, state.astype(BF16), preferred_element_type=F32) * q_decay
        kd_t = (k.astype(F32) * k_decay).T.astype(BF16)
        state_ref[...] = state * block_decay + jnp.dot(kd_t, v, preferred_element_type=F32)
        mu = jnp.mean(y, axis=-1, keepdims=True)
        yc = y - mu
        var = jnp.mean(yc * yc, axis=-1, keepdims=True)
        gated = g_ref[pl.ds(r0, cb), :].astype(F32) * lax.rsqrt(var + GN_EPS)
        o_ref[pl.ds(r0, cb), :] = (gated * yc).astype(o_ref.dtype)
        return carry

    lax.fori_loop(0, n_inner, body, 0)


def _retention(qkvg, log_gamma, heads, dk, dv, tb=2048, cb=RET_BLOCK):
    s = qkvg.shape[0]
    k_col0 = heads
    v_col0 = 2 * heads * dk // dv
    g_col0 = v_col0 + heads
    kern = functools.partial(_retention_kernel, cb=cb, n_inner=tb // cb)
    return pl.pallas_call(
        kern,
        grid=(heads, s // tb),
        in_specs=[pl.BlockSpec(memory_space=pltpu.SMEM),
                  pl.BlockSpec((tb, dk), lambda h, t: (t, h)),
                  pl.BlockSpec((tb, dk), lambda h, t: (t, k_col0 + h)),
                  pl.BlockSpec((tb, dv), lambda h, t: (t, v_col0 + h)),
                  pl.BlockSpec((tb, dv), lambda h, t: (t, g_col0 + h))],
        out_specs=pl.BlockSpec((tb, dv), lambda h, t: (t, h)),
        out_shape=jax.ShapeDtypeStruct((s, heads * dv), BF16),
        scratch_shapes=[pltpu.VMEM((dk, dv), F32), pltpu.VMEM((cb, cb), F32)],
        compiler_params=_cparams(2),
        name="retention",
    )(log_gamma, qkvg, qkvg, qkvg, qkvg)


def _outproj_ple_kernel(z_ref, h_ref, p_ref, wout_ref, pn_ref, wgate_ref, wproj_ref,
                        nn_ref, *out_refs, final):
    h1 = h_ref[...] + jnp.dot(z_ref[...], wout_ref[...], preferred_element_type=F32)
    xn = _rms(h1, pn_ref[...]).astype(BF16)
    gate = _sigmoid(jnp.dot(xn, wgate_ref[...], preferred_element_type=F32))
    emb = jnp.dot(p_ref[...].astype(BF16), wproj_ref[...], preferred_element_type=F32)
    h2 = h1 + gate * emb
    nxt = _rms(h2, nn_ref[...])
    if final:
        out_refs[0][...] = nxt
    else:
        out_refs[0][...] = h2
        out_refs[1][...] = nxt.astype(BF16)


def _outproj_ple(z, h, p, layer, w_out, ple_norm, w_gate, w_proj, next_norm, final, tm=256):
    s, d = h.shape[-2:]
    kz = z.shape[1]
    pd = p.shape[-1]
    row = lambda i: (i, 0)
    resident = lambda shape, idx: pl.BlockSpec(shape, lambda i: idx, pipeline_mode=pl.Buffered(1))
    h_spec = pl.BlockSpec((tm, d), row) if h.ndim == 2 else pl.BlockSpec((None, tm, d), lambda i: (0, i, 0))
    if final:
        out_shape = (jax.ShapeDtypeStruct((1, s, d), F32),)
        out_specs = (pl.BlockSpec((None, tm, d), lambda i: (0, i, 0)),)
    else:
        out_shape = (jax.ShapeDtypeStruct((s, d), F32), jax.ShapeDtypeStruct((s, d), BF16))
        out_specs = (pl.BlockSpec((tm, d), row), pl.BlockSpec((tm, d), row))
    return pl.pallas_call(
        functools.partial(_outproj_ple_kernel, final=final),
        grid=(s // tm,),
        in_specs=[pl.BlockSpec((tm, kz), row),
                  h_spec,
                  pl.BlockSpec((None, None, tm, pd), lambda i: (layer, 0, i, 0)),
                  resident((None, kz, d), (0, 0, 0)),
                  resident((None, 1, d), (layer, 0, 0)),
                  resident((None, d, d), (layer, 0, 0)),
                  resident((None, pd, d), (layer, 0, 0)),
                  resident((1, d), (0, 0))],
        out_specs=out_specs,
        out_shape=out_shape,
        compiler_params=_cparams(1),
        name="outproj_ple_final" if final else "outproj_ple",
    )(z, h, p, w_out, ple_norm, w_gate, w_proj, next_norm)


def _pool_kernel(xn_ref, wu_ref, wg_ref, wgrp_ref, scale_ref, o_ref,
                 ubuf_ref, s2_ref, s4_ref, s8_ref):
    grp = pl.program_id(0)
    i = pl.program_id(1)
    tm = xn_ref.shape[0]
    hl = POOL_HALO
    end = hl + tm

    @pl.when(i == 0)
    def _():
        ubuf_ref[0:hl, :] = jnp.zeros((hl, ubuf_ref.shape[1]), F32)

    xn = xn_ref[...]
    u = jnp.dot(xn, wu_ref[...], preferred_element_type=F32)
    gate = jnp.dot(xn, wg_ref[...], preferred_element_type=F32)

    ubuf_ref[hl:end, :] = u
    s2_ref[8:end, :] = ubuf_ref[8:end, :] + ubuf_ref[7:end - 1, :]
    s4_ref[16:end, :] = s2_ref[16:end, :] + s2_ref[14:end - 2, :]
    s8_ref[24:end, :] = s4_ref[24:end, :] + s4_ref[20:end - 4, :]
    s16 = s8_ref[hl:end, :] + s8_ref[hl - 8:end - 8, :]
    wsum = jnp.where(grp == 0, s2_ref[hl:end, :],
                     jnp.where(grp == 1, s4_ref[hl:end, :],
                               jnp.where(grp == 2, s8_ref[hl:end, :], s16)))
    ubuf_ref[0:hl, :] = ubuf_ref[tm:end, :]

    window = jnp.left_shift(2, grp)
    tok = i * tm + lax.broadcasted_iota(jnp.int32, (tm, 1), 0)
    inv_cnt = 1.0 / jnp.minimum(tok + 1, window).astype(F32)
    dev = (wsum * inv_cnt - u).astype(BF16)
    mixed = jnp.dot(dev, wgrp_ref[...], preferred_element_type=F32) * scale_ref[...]
    o_ref[...] = (mixed * _silu(gate)).astype(o_ref.dtype)


def _pool_mixer(xn, w_in, w_group, scale, tm=512):
    s, d = xn.shape
    _, n_groups, gd, _ = w_group.shape
    width = n_groups * gd
    assert tuple(2 << g for g in range(n_groups)) == POOL_WINDOWS
    halo_buf = pltpu.VMEM((POOL_HALO + tm, gd), F32)
    return pl.pallas_call(
        _pool_kernel,
        grid=(n_groups, s // tm),
        in_specs=[pl.BlockSpec((tm, d), lambda g, i: (i, 0)),
                  pl.BlockSpec((None, d, gd), lambda g, i: (0, 0, g)),
                  pl.BlockSpec((None, d, gd), lambda g, i: (0, 0, n_groups + g)),
                  pl.BlockSpec((None, None, gd, gd), lambda g, i: (0, g, 0, 0)),
                  pl.BlockSpec((1, gd), lambda g, i: (0, g))],
        out_specs=pl.BlockSpec((tm, gd), lambda g, i: (i, g)),
        out_shape=jax.ShapeDtypeStruct((s, width), BF16),
        scratch_shapes=[halo_buf, halo_buf, halo_buf, halo_buf],
        compiler_params=_cparams(2),
        name="pool_mixer",
    )(xn, w_in, w_in, w_group, scale)


def _rotary_tables(seq, dk, tm):
    freq = ROPE_BASE ** (-jnp.linspace(0.0, 1.0, dk // 2, dtype=F32))
    freq2 = jnp.repeat(freq, 2)
    ang_r = jnp.arange(tm, dtype=F32)[:, None] * freq2[None, :]
    ang_t = (jnp.arange(seq // tm, dtype=F32) * tm)[:, None, None] * freq2[None, None, :]
    return jnp.cos(ang_r), jnp.sin(ang_r), jnp.cos(ang_t), jnp.sin(ang_t)


def kernel(x, p, ret_norm, ret_w_in, ret_w_out, pool_norm, pool_w_in, pool_w_group, pool_scale,
           pool_w_out, ple_norm, ple_w_gate, ple_w_proj, final_norm):
    b, s, d = x.shape
    assert b == 1
    heads = RET_HEADS
    dk = d // heads
    dv = ret_w_out.shape[1] // heads
    assert ret_w_in.shape[2] == 2 * heads * dk + 2 * heads * dv
    tm_inproj = 1024

    tabs = _rotary_tables(s, dk, tm_inproj)
    log_gamma = jnp.log1p(-(2.0 ** (-5.0 - jnp.arange(heads, dtype=F32))))
    ple_norm3 = ple_norm.reshape(ple_norm.shape[0], 1, d)
    w_gate = ple_w_gate.astype(BF16)
    w_proj = ple_w_proj.astype(BF16)

    xn0 = _rmsnorm(x, ret_norm)
    qkvg = _ret_inproj(xn0, ret_w_in, tabs, heads * dk, heads * dv, dk ** -0.5, tm_inproj)
    yg = _retention(qkvg, log_gamma, heads, dk, dv)
    h2, xn2 = _outproj_ple(yg, x, p, 0, ret_w_out.astype(BF16), ple_norm3, w_gate, w_proj,
                           pool_norm, final=False)

    z = _pool_mixer(xn2, pool_w_in.astype(BF16), pool_w_group.astype(BF16), pool_scale)
    (out,) = _outproj_ple(z, h2, p, 1, pool_w_out.astype(BF16), ple_norm3, w_gate, w_proj,
                          final_norm.reshape(1, d), final=True)
    return out
```

```python
import functools

import jax
import jax.numpy as jnp
from jax import lax
from jax.experimental import pallas as pl
from jax.experimental.pallas import tpu as pltpu

F32 = jnp.float32
BF16 = jnp.bfloat16

RET_HEADS = 8
ROPE_BASE = 10000.0
POOL_WINDOWS = (2, 4, 8, 16)
NORM_EPS = 1e-6
GN_EPS = 1e-5

LANES = 128
SUBLANES = 8
MXU_WIDTH = 256
POOL_HALO = 32
RET_BLOCK = 256
VMEM_LIMIT = 56 * 1024 * 1024


def _cparams(n_axes):
    return pltpu.CompilerParams(
        dimension_semantics=("arbitrary",) * n_axes,
        vmem_limit_bytes=VMEM_LIMIT,
    )


def _rms(x, gain):
    ms = jnp.mean(x * x, axis=-1, keepdims=True)
    return x * lax.rsqrt(ms + NORM_EPS) * gain


def _sigmoid(x):
    return 0.5 * jnp.tanh(0.5 * x) + 0.5


def _silu(x):
    return x * _sigmoid(x)


def _rmsnorm_kernel(x_ref, g_ref, o_ref):
    o_ref[...] = _rms(x_ref[...], g_ref[...]).astype(o_ref.dtype)


def _rmsnorm(x, gain, tm=1024):
    _, s, d = x.shape
    return pl.pallas_call(
        _rmsnorm_kernel,
        grid=(s // tm,),
        in_specs=[pl.BlockSpec((None, tm, d), lambda i: (0, i, 0)),
                  pl.BlockSpec((1, d), lambda i: (0, 0))],
        out_specs=pl.BlockSpec((tm, d), lambda i: (i, 0)),
        out_shape=jax.ShapeDtypeStruct((s, d), BF16),
        compiler_params=_cparams(1),
        name="rmsnorm",
    )(x, gain)


def _ret_inproj_kernel(xn_ref, w_ref, cr_ref, sr_ref, ct_ref, st_ref, *rest,
                       n_side, n_q_tiles, n_rot_tiles, n_plain_tiles, k_scale):
    side_in = rest[:n_side]
    o_ref = rest[n_side]
    side_out = rest[n_side + 1:2 * n_side + 1]
    wbf_ref, cs_ref, sn_ref = rest[2 * n_side + 1:]
    n = pl.program_id(0)
    i = pl.program_id(1)
    tn = o_ref.shape[1]

    @pl.when(i == 0)
    def _():
        wbf_ref[...] = w_ref[...].astype(BF16)

    def run(epilogue):
        for src, dst in zip(side_in, side_out):
            dst[...] = src[...].astype(BF16)
        for c in range(tn // MXU_WIDTH):
            cols = slice(c * MXU_WIDTH, (c + 1) * MXU_WIDTH)
            acc = jnp.dot(xn_ref[...], wbf_ref[:, cols], preferred_element_type=F32)
            o_ref[:, cols] = epilogue(acc).astype(o_ref.dtype)

    @pl.when(n < n_rot_tiles)
    def _():
        scale = jnp.where(n < n_q_tiles, 1.0, k_scale).astype(F32)
        ct, st = ct_ref[...] * scale, st_ref[...] * scale
        lane_t = lax.broadcasted_iota(jnp.int32, ct.shape, 1)
        st_signed = jnp.where((lane_t & 1) == 0, -st, st)
        ct_signed = jnp.where((lane_t & 1) == 0, -ct, ct)
        cr, sr = cr_ref[...], sr_ref[...]
        cs_ref[...] = ct * cr - st * sr
        sn_ref[...] = st_signed * cr + ct_signed * sr
        lane = lax.broadcasted_iota(jnp.int32, (xn_ref.shape[0], LANES), 1)
        even = (lane & 1) == 0

        def rotate(acc):
            outs = []
            for hcol in range(0, MXU_WIDTH, LANES):
                xs = acc[:, hcol:hcol + LANES]
                partner = jnp.where(even, pltpu.roll(xs, LANES - 1, 1), pltpu.roll(xs, 1, 1))
                outs.append(xs * cs_ref[:, hcol:hcol + LANES] + partner * sn_ref[:, hcol:hcol + LANES])
            return jnp.concatenate(outs, axis=1)

        run(rotate)

    @pl.when((n >= n_rot_tiles) & (n < n_plain_tiles))
    def _():
        run(lambda acc: acc)

    @pl.when(n >= n_plain_tiles)
    def _():
        run(_silu)


def _ret_inproj(xn, w_in, tabs, side_weights, qk_width, v_width, k_scale, tm, tn=1024):
    s, d = xn.shape
    n_out = w_in.shape[2]
    cr, sr, ct, st = tabs
    head_w = cr.shape[1]
    assert head_w == MXU_WIDTH and cr.shape[0] == tm
    n_i = s // tm
    n_rot_tiles = 2 * qk_width // tn
    kern = functools.partial(_ret_inproj_kernel, n_side=len(side_weights), n_q_tiles=qk_width // tn,
                             n_rot_tiles=n_rot_tiles, n_plain_tiles=n_rot_tiles + v_width // tn,
                             k_scale=k_scale)
    side_specs, side_shapes = [], []
    for arr, chunk_rows in side_weights:
        rows, cols = arr.shape
        n_chunks = rows // chunk_rows
        assert n_chunks * chunk_rows == rows and n_chunks <= (n_out // tn) * n_i
        chunk_map = lambda n, i, last=n_chunks - 1: (jnp.minimum(n * n_i + i, last), 0)
        side_specs.append(pl.BlockSpec((chunk_rows, cols), chunk_map))
        side_shapes.append(jax.ShapeDtypeStruct((rows, cols), BF16))
    outs = pl.pallas_call(
        kern,
        grid=(n_out // tn, n_i),
        in_specs=[pl.BlockSpec((tm, d), lambda n, i: (i, 0)),
                  pl.BlockSpec((None, d, tn), lambda n, i: (0, 0, n)),
                  pl.BlockSpec((tm, head_w), lambda n, i: (0, 0)),
                  pl.BlockSpec((tm, head_w), lambda n, i: (0, 0)),
                  pl.BlockSpec((None, 1, head_w), lambda n, i: (i, 0, 0)),
                  pl.BlockSpec((None, 1, head_w), lambda n, i: (i, 0, 0))] + side_specs,
        out_specs=[pl.BlockSpec((tm, tn), lambda n, i: (i, n))] + side_specs,
        out_shape=[jax.ShapeDtypeStruct((s, n_out), BF16)] + side_shapes,
        scratch_shapes=[pltpu.VMEM((d, tn), BF16),
                        pltpu.VMEM((tm, head_w), F32),
                        pltpu.VMEM((tm, head_w), F32)],
        compiler_params=_cparams(2),
        name="ret_inproj",
    )(xn, w_in, cr, sr, ct, st, *[arr for arr, _ in side_weights])
    return outs[0], outs[1:]


def _retention_kernel(lg_ref, q_ref, k_ref, v_ref, g_ref, o_ref, state_ref, dmat_ref, *,
                      cb, n_inner):
    h = pl.program_id(0)
    t = pl.program_id(1)
    lg = lg_ref[h]

    @pl.when(t == 0)
    def _():
        state_ref[...] = jnp.zeros_like(state_ref)
        r = lax.broadcasted_iota(jnp.int32, (cb, cb), 0)
        c = lax.broadcasted_iota(jnp.int32, (cb, cb), 1)
        diff = (r - c).astype(F32)
        dmat_ref[...] = jnp.where(diff >= 0, jnp.exp(jnp.maximum(diff, 0.0) * lg), 0.0)

    idx = lax.broadcasted_iota(jnp.int32, (cb, 1), 0).astype(F32)
    q_decay = jnp.exp((idx + 1.0) * lg)
    k_decay = jnp.exp((cb - 1.0 - idx) * lg)
    block_decay = jnp.exp(jnp.full((1, 1), cb, F32) * lg)

    for c in range(n_inner):
        rows = slice(c * cb, (c + 1) * cb)
        q = q_ref[rows, :]
        k = k_ref[rows, :]
        v = v_ref[rows, :]
        scores = lax.dot_general(q, k, (((1,), (1,)), ((), ())), preferred_element_type=F32)
        p = (scores * dmat_ref[...]).astype(BF16)
        state = state_ref[...]
        qd = (q.astype(F32) * q_decay).astype(BF16)
        y = (jnp.dot(p, v, preferred_element_type=F32)
             + jnp.dot(qd, state.astype(BF16), preferred_element_type=F32))
        kd_t = (k.astype(F32) * k_decay).T.astype(BF16)
        state_ref[...] = state * block_decay + jnp.dot(kd_t, v, preferred_element_type=F32)
        mu = jnp.mean(y, axis=-1, keepdims=True)
        yc = y - mu
        var = jnp.mean(yc * yc, axis=-1, keepdims=True)
        yn = (yc * lax.rsqrt(var + GN_EPS)).astype(BF16)
        o_ref[rows, :] = yn * g_ref[rows, :]


def _retention(qkvg, log_gamma, heads, dk, dv, tb=2048, cb=RET_BLOCK):
    s = qkvg.shape[0]
    k_col0 = heads
    v_col0 = 2 * heads * dk // dv
    g_col0 = v_col0 + heads
    kern = functools.partial(_retention_kernel, cb=cb, n_inner=tb // cb)
    return pl.pallas_call(
        kern,
        grid=(heads, s // tb),
        in_specs=[pl.BlockSpec(memory_space=pltpu.SMEM),
                  pl.BlockSpec((tb, dk), lambda h, t: (t, h)),
                  pl.BlockSpec((tb, dk), lambda h, t: (t, k_col0 + h)),
                  pl.BlockSpec((tb, dv), lambda h, t: (t, v_col0 + h)),
                  pl.BlockSpec((tb, dv), lambda h, t: (t, g_col0 + h))],
        out_specs=pl.BlockSpec((tb, dv), lambda h, t: (t, h)),
        out_shape=jax.ShapeDtypeStruct((s, heads * dv), BF16),
        scratch_shapes=[pltpu.VMEM((dk, dv), F32), pltpu.VMEM((cb, cb), F32)],
        compiler_params=_cparams(2),
        name="retention",
    )(log_gamma, qkvg, qkvg, qkvg, qkvg)


def _outproj_ple_kernel(z_ref, h_ref, p_ref, wout_ref, pn_ref, wgate_ref, wproj_ref,
                        nn_ref, *out_refs, final):
    h1 = h_ref[...] + jnp.dot(z_ref[...], wout_ref[...], preferred_element_type=F32)
    xn = _rms(h1, pn_ref[...]).astype(BF16)
    gate = _sigmoid(jnp.dot(xn, wgate_ref[...], preferred_element_type=F32))
    emb = jnp.dot(p_ref[...].astype(BF16), wproj_ref[...], preferred_element_type=F32)
    h2 = h1 + gate * emb
    nxt = _rms(h2, nn_ref[...])
    if final:
        out_refs[0][...] = nxt
    else:
        out_refs[0][...] = h2
        out_refs[1][...] = nxt.astype(BF16)


def _outproj_ple(z, h, p, layer, w_out, ple_norm, w_gate, w_proj, next_norm, final, tm=256):
    s, d = h.shape[-2:]
    kz = z.shape[1]
    pd = p.shape[-1]
    row = lambda i: (i, 0)
    resident = lambda shape, idx: pl.BlockSpec(shape, lambda i: idx, pipeline_mode=pl.Buffered(1))
    h_spec = pl.BlockSpec((tm, d), row) if h.ndim == 2 else pl.BlockSpec((None, tm, d), lambda i: (0, i, 0))
    if final:
        out_shape = (jax.ShapeDtypeStruct((1, s, d), F32),)
        out_specs = (pl.BlockSpec((None, tm, d), lambda i: (0, i, 0)),)
    else:
        out_shape = (jax.ShapeDtypeStruct((s, d), F32), jax.ShapeDtypeStruct((s, d), BF16))
        out_specs = (pl.BlockSpec((tm, d), row), pl.BlockSpec((tm, d), row))
    return pl.pallas_call(
        functools.partial(_outproj_ple_kernel, final=final),
        grid=(s // tm,),
        in_specs=[pl.BlockSpec((tm, kz), row),
                  h_spec,
                  pl.BlockSpec((None, None, tm, pd), lambda i: (layer, 0, i, 0)),
                  resident((kz, d), (0, 0)),
                  resident((None, 1, d), (layer, 0, 0)),
                  resident((None, d, d), (layer, 0, 0)),
                  resident((None, pd, d), (layer, 0, 0)),
                  resident((1, d), (0, 0))],
        out_specs=out_specs,
        out_shape=out_shape,
        compiler_params=_cparams(1),
        name="outproj_ple_final" if final else "outproj_ple",
    )(z, h, p, w_out, ple_norm, w_gate, w_proj, next_norm)


def _pool_kernel(xn_ref, wu_ref, wg_ref, wgrp_ref, scale_ref, o_ref,
                 ubuf_ref, s2_ref, s4_ref, s8_ref):
    grp = pl.program_id(0)
    i = pl.program_id(1)
    tm = xn_ref.shape[0]
    hl = POOL_HALO
    end = hl + tm

    @pl.when(i == 0)
    def _():
        ubuf_ref[0:hl, :] = jnp.zeros((hl, ubuf_ref.shape[1]), F32)

    xn = xn_ref[...]
    u = jnp.dot(xn, wu_ref[...], preferred_element_type=F32)
    gate = jnp.dot(xn, wg_ref[...], preferred_element_type=F32)

    ubuf_ref[hl:end, :] = u
    s2_ref[8:end, :] = ubuf_ref[8:end, :] + ubuf_ref[7:end - 1, :]
    s4_ref[16:end, :] = s2_ref[16:end, :] + s2_ref[14:end - 2, :]
    s8_ref[24:end, :] = s4_ref[24:end, :] + s4_ref[20:end - 4, :]
    s16 = s8_ref[hl:end, :] + s8_ref[hl - 8:end - 8, :]
    wsum = jnp.where(grp == 0, s2_ref[hl:end, :],
                     jnp.where(grp == 1, s4_ref[hl:end, :],
                               jnp.where(grp == 2, s8_ref[hl:end, :], s16)))
    ubuf_ref[0:hl, :] = ubuf_ref[tm:end, :]

    window = jnp.left_shift(2, grp)
    tok = i * tm + lax.broadcasted_iota(jnp.int32, (tm, 1), 0)
    inv_cnt = 1.0 / jnp.minimum(tok + 1, window).astype(F32)
    dev = (wsum * inv_cnt - u).astype(BF16)
    mixed = jnp.dot(dev, wgrp_ref[...], preferred_element_type=F32) * scale_ref[...]
    o_ref[...] = (mixed * _silu(gate)).astype(o_ref.dtype)


def _pool_mixer(xn, w_in, w_group, scale, tm=512):
    s, d = xn.shape
    n_groups, gd, _ = w_group.shape
    width = n_groups * gd
    assert tuple(2 << g for g in range(n_groups)) == POOL_WINDOWS
    halo_buf = pltpu.VMEM((POOL_HALO + tm, gd), F32)
    return pl.pallas_call(
        _pool_kernel,
        grid=(n_groups, s // tm),
        in_specs=[pl.BlockSpec((tm, d), lambda g, i: (i, 0)),
                  pl.BlockSpec((d, gd), lambda g, i: (0, g)),
                  pl.BlockSpec((d, gd), lambda g, i: (0, n_groups + g)),
                  pl.BlockSpec((None, gd, gd), lambda g, i: (g, 0, 0)),
                  pl.BlockSpec((1, gd), lambda g, i: (0, g))],
        out_specs=pl.BlockSpec((tm, gd), lambda g, i: (i, g)),
        out_shape=jax.ShapeDtypeStruct((s, width), BF16),
        scratch_shapes=[halo_buf, halo_buf, halo_buf, halo_buf],
        compiler_params=_cparams(2),
        name="pool_mixer",
    )(xn, w_in, w_in, w_group, scale)


def _rotary_tables(seq, dk, tm):
    freq = ROPE_BASE ** (-jnp.linspace(0.0, 1.0, dk // 2, dtype=F32))
    freq2 = jnp.repeat(freq, 2)
    ang_r = jnp.arange(tm, dtype=F32)[:, None] * freq2[None, :]
    ang_t = (jnp.arange(seq // tm, dtype=F32) * tm)[:, None, None] * freq2[None, None, :]
    return jnp.cos(ang_r), jnp.sin(ang_r), jnp.cos(ang_t), jnp.sin(ang_t)


def kernel(x, p, ret_norm, ret_w_in, ret_w_out, pool_norm, pool_w_in, pool_w_group, pool_scale,
           pool_w_out, ple_norm, ple_w_gate, ple_w_proj, final_norm):
    b, s, d = x.shape
    assert b == 1
    heads = RET_HEADS
    dk = d // heads
    dv = ret_w_out.shape[1] // heads
    assert ret_w_in.shape[2] == 2 * heads * dk + 2 * heads * dv
    tm_inproj = 1024

    tabs = _rotary_tables(s, dk, tm_inproj)
    log_gamma = jnp.log1p(-(2.0 ** (-5.0 - jnp.arange(heads, dtype=F32))))
    ple_norm3 = ple_norm.reshape(ple_norm.shape[0], 1, d)
    n_layers, pd, _ = ple_w_proj.shape
    n_groups, gd = pool_w_group.shape[1:3]
    pool_width = n_groups * gd

    side = [(ret_w_out.reshape(heads * dv, d), 64),
            (pool_w_in.reshape(d, 2 * pool_width), 32),
            (pool_w_group.reshape(pool_width, gd), 64),
            (pool_w_out.reshape(pool_width, d), 64),
            (ple_w_gate.reshape(n_layers * d, d), 64),
            (ple_w_proj.reshape(n_layers * pd, d), 32)]

    xn0 = _rmsnorm(x, ret_norm)
    qkvg, (w_ret_out, w_pool_in, w_pool_group, w_pool_out, w_gate, w_proj) = _ret_inproj(
        xn0, ret_w_in, tabs, side, heads * dk, heads * dv, dk ** -0.5, tm_inproj)
    w_gate = w_gate.reshape(n_layers, d, d)
    w_proj = w_proj.reshape(n_layers, pd, d)
    yg = _retention(qkvg, log_gamma, heads, dk, dv)
    h2, xn2 = _outproj_ple(yg, x, p, 0, w_ret_out, ple_norm3, w_gate, w_proj, pool_norm, final=False)

    z = _pool_mixer(xn2, w_pool_in, w_pool_group.reshape(n_groups, gd, gd), pool_scale)
    (out,) = _outproj_ple(z, h2, p, 1, w_pool_out, ple_norm3, w_gate, w_proj,
                          final_norm.reshape(1, d), final=True)
    return out
```

```python
import functools

import jax
import jax.numpy as jnp
from jax import lax
from jax.experimental import pallas as pl
from jax.experimental.pallas import tpu as pltpu

F32 = jnp.float32
BF16 = jnp.bfloat16

RET_HEADS = 8
ROPE_BASE = 10000.0
POOL_WINDOWS = (2, 4, 8, 16)
NORM_EPS = 1e-6
GN_EPS = 1e-5

LANES = 128
SUBLANES = 8
MXU_WIDTH = 256
POOL_HALO = 32
RET_BLOCK = 256
VMEM_LIMIT = 56 * 1024 * 1024


def _cparams(n_axes):
    return pltpu.CompilerParams(
        dimension_semantics=("arbitrary",) * n_axes,
        vmem_limit_bytes=VMEM_LIMIT,
    )


def _rms(x, gain):
    ms = jnp.mean(x * x, axis=-1, keepdims=True)
    return x * lax.rsqrt(ms + NORM_EPS) * gain


def _sigmoid(x):
    return 0.5 * jnp.tanh(0.5 * x) + 0.5


def _silu(x):
    return x * _sigmoid(x)


def _rmsnorm_kernel(x_ref, g_ref, o_ref):
    o_ref[...] = _rms(x_ref[...], g_ref[...]).astype(o_ref.dtype)


def _rmsnorm(x, gain, tm=1024):
    _, s, d = x.shape
    return pl.pallas_call(
        _rmsnorm_kernel,
        grid=(s // tm,),
        in_specs=[pl.BlockSpec((None, tm, d), lambda i: (0, i, 0)),
                  pl.BlockSpec((1, d), lambda i: (0, 0))],
        out_specs=pl.BlockSpec((tm, d), lambda i: (i, 0)),
        out_shape=jax.ShapeDtypeStruct((s, d), BF16),
        compiler_params=_cparams(1),
        name="rmsnorm",
    )(x, gain)


def _ret_inproj_kernel(lg_ref, xn_ref, w_ref, cr_ref, sr_ref, ct_ref, st_ref, *rest,
                       n_side, heads, n_q_tiles, n_rot_tiles, n_plain_tiles, k_scale, cb):
    side_in = rest[:n_side]
    o_ref = rest[n_side]
    side_out = rest[n_side + 1:2 * n_side + 1]
    wbf_ref, cs_ref, sn_ref = rest[2 * n_side + 1:]
    n = pl.program_id(0)
    i = pl.program_id(1)
    tm, tn = o_ref.shape
    heads_per_tile = tn // MXU_WIDTH

    @pl.when((i == 0) & (n >= n_rot_tiles))
    def _():
        wbf_ref[...] = w_ref[...].astype(BF16)

    @pl.when((i == 0) & (n < n_rot_tiles))
    def _():
        src = lax.broadcasted_iota(jnp.int32, (MXU_WIDTH, MXU_WIDTH), 0)
        dst = lax.broadcasted_iota(jnp.int32, (MXU_WIDTH, MXU_WIDTH), 1)
        half = MXU_WIDTH // 2
        wanted = jnp.where(dst < half, 2 * dst, 2 * (dst - half) + 1)
        perm = (src == wanted).astype(BF16)
        for c in range(heads_per_tile):
            cols = slice(c * MXU_WIDTH, (c + 1) * MXU_WIDTH)
            wbf_ref[:, cols] = jnp.dot(w_ref[:, cols].astype(BF16), perm,
                                       preferred_element_type=F32).astype(BF16)

    def run(epilogue):
        for src, dst in zip(side_in, side_out):
            dst[...] = src[...].astype(BF16)
        for c in range(heads_per_tile):
            cols = slice(c * MXU_WIDTH, (c + 1) * MXU_WIDTH)
            acc = jnp.dot(xn_ref[...], wbf_ref[:, cols], preferred_element_type=F32)
            o_ref[:, cols] = epilogue(acc, c).astype(o_ref.dtype)

    @pl.when(n < n_rot_tiles)
    def _():
        ct, st = ct_ref[...], st_ref[...]
        cr, sr = cr_ref[...], sr_ref[...]
        cs_ref[...] = ct * cr - st * sr
        sn_ref[...] = st * cr + ct * sr
        is_k = n >= n_q_tiles
        row = lax.broadcasted_iota(jnp.int32, (tm, LANES), 0)
        j = (row & (cb - 1)).astype(F32)
        expo = jnp.where(is_k, cb - 1.0 - j, j + 1.0)
        scale = jnp.where(is_k, k_scale, 1.0).astype(F32)

        def rotate(acc, c):
            head = (n * heads_per_tile + c) % heads
            decay = jnp.exp(expo * lg_ref[head]) * scale
            cd = cs_ref[...] * decay
            sd = sn_ref[...] * decay
            xe, xo = acc[:, :LANES], acc[:, LANES:]
            return jnp.concatenate([xe * cd - xo * sd, xo * cd + xe * sd], axis=1)

        run(rotate)

    @pl.when((n >= n_rot_tiles) & (n < n_plain_tiles))
    def _():
        run(lambda acc, c: acc)

    @pl.when(n >= n_plain_tiles)
    def _():
        run(lambda acc, c: _silu(acc))


def _ret_inproj(xn, w_in, tabs, log_gamma, side_weights, heads, qk_width, v_width, k_scale, tm,
                tn=1024, cb=RET_BLOCK):
    s, d = xn.shape
    n_out = w_in.shape[2]
    cr, sr, ct, st = tabs
    head_w = cr.shape[1]
    assert 2 * head_w == MXU_WIDTH == 2 * LANES and cr.shape[0] == tm
    assert tm % cb == 0 and cb & (cb - 1) == 0
    n_i = s // tm
    n_rot_tiles = 2 * qk_width // tn
    kern = functools.partial(_ret_inproj_kernel, n_side=len(side_weights), heads=heads,
                             n_q_tiles=qk_width // tn, n_rot_tiles=n_rot_tiles,
                             n_plain_tiles=n_rot_tiles + v_width // tn, k_scale=k_scale, cb=cb)
    side_specs, side_shapes = [], []
    for arr, chunk_rows in side_weights:
        rows, cols = arr.shape
        n_chunks = rows // chunk_rows
        assert n_chunks * chunk_rows == rows and n_chunks <= (n_out // tn) * n_i
        chunk_map = lambda n, i, last=n_chunks - 1: (jnp.minimum(n * n_i + i, last), 0)
        side_specs.append(pl.BlockSpec((chunk_rows, cols), chunk_map))
        side_shapes.append(jax.ShapeDtypeStruct((rows, cols), BF16))
    outs = pl.pallas_call(
        kern,
        grid=(n_out // tn, n_i),
        in_specs=[pl.BlockSpec(memory_space=pltpu.SMEM),
                  pl.BlockSpec((tm, d), lambda n, i: (i, 0)),
                  pl.BlockSpec((None, d, tn), lambda n, i: (0, 0, n)),
                  pl.BlockSpec((tm, head_w), lambda n, i: (0, 0)),
                  pl.BlockSpec((tm, head_w), lambda n, i: (0, 0)),
                  pl.BlockSpec((None, 1, head_w), lambda n, i: (i, 0, 0)),
                  pl.BlockSpec((None, 1, head_w), lambda n, i: (i, 0, 0))] + side_specs,
        out_specs=[pl.BlockSpec((tm, tn), lambda n, i: (i, n))] + side_specs,
        out_shape=[jax.ShapeDtypeStruct((s, n_out), BF16)] + side_shapes,
        scratch_shapes=[pltpu.VMEM((d, tn), BF16),
                        pltpu.VMEM((tm, head_w), F32),
                        pltpu.VMEM((tm, head_w), F32)],
        compiler_params=_cparams(2),
        name="ret_inproj",
    )(log_gamma, xn, w_in, cr, sr, ct, st, *[arr for arr, _ in side_weights])
    return outs[0], outs[1:]


def _ret_outproj_kernel(lg_ref, qkvg_ref, x_ref, wout_ref, o_ref, state_ref, dmat_ref, *,
                        heads, dk, dv):
    t = pl.program_id(0)
    cb = qkvg_ref.shape[0]

    @pl.when(t == 0)
    def _():
        state_ref[...] = jnp.zeros_like(state_ref)
        r = lax.broadcasted_iota(jnp.int32, (cb, cb), 0)
        c = lax.broadcasted_iota(jnp.int32, (cb, cb), 1)
        for h in range(heads):
            undo = jnp.exp(jnp.full((cb, cb), -cb, F32) * lg_ref[h])
            dmat_ref[h] = jnp.where(r >= c, undo, 0.0)

    k0, v0, g0 = heads * dk, 2 * heads * dk, 2 * heads * dk + heads * dv

    def mix(h):
        qd = qkvg_ref[:, h * dk:(h + 1) * dk]
        kd = qkvg_ref[:, k0 + h * dk:k0 + (h + 1) * dk]
        v = qkvg_ref[:, v0 + h * dv:v0 + (h + 1) * dv]
        block_decay = jnp.exp(jnp.full((1, 1), cb, F32) * lg_ref[h])
        scores = lax.dot_general(qd, kd, (((1,), (1,)), ((), ())), preferred_element_type=F32)
        p = (scores * dmat_ref[h]).astype(BF16)
        state = state_ref[h]
        y = (jnp.dot(p, v, preferred_element_type=F32)
             + jnp.dot(qd, state.astype(BF16), preferred_element_type=F32))
        state_ref[h] = state * block_decay + lax.dot_general(
            kd, v, (((0,), (0,)), ((), ())), preferred_element_type=F32)
        return y

    def norm_gate(h, y):
        gate = qkvg_ref[:, g0 + h * dv:g0 + (h + 1) * dv]
        mu = jnp.mean(y, axis=-1, keepdims=True)
        yc = y - mu
        var = jnp.mean(yc * yc, axis=-1, keepdims=True)
        return (yc * lax.rsqrt(var + GN_EPS)).astype(BF16) * gate

    def project(h, yg, acc):
        return acc + jnp.dot(yg, wout_ref[h * dv:(h + 1) * dv, :], preferred_element_type=F32)

    acc = x_ref[...]
    ys, ygs = {}, {}
    for step in range(heads + 2):
        if step < heads:
            ys[step] = mix(step)
        if 0 <= step - 1 < heads:
            ygs[step - 1] = norm_gate(step - 1, ys.pop(step - 1))
        if 0 <= step - 2 < heads:
            acc = project(step - 2, ygs.pop(step - 2), acc)
    o_ref[...] = acc


def _ret_outproj(qkvg, x, w_out, log_gamma, heads, dk, dv, cb=RET_BLOCK):
    s, n_qkvg = qkvg.shape
    d = w_out.shape[1]
    return pl.pallas_call(
        functools.partial(_ret_outproj_kernel, heads=heads, dk=dk, dv=dv),
        grid=(s // cb,),
        in_specs=[pl.BlockSpec(memory_space=pltpu.SMEM),
                  pl.BlockSpec((cb, n_qkvg), lambda t: (t, 0)),
                  pl.BlockSpec((None, cb, d), lambda t: (0, t, 0)),
                  pl.BlockSpec((heads * dv, d), lambda t: (0, 0), pipeline_mode=pl.Buffered(1))],
        out_specs=pl.BlockSpec((cb, d), lambda t: (t, 0)),
        out_shape=jax.ShapeDtypeStruct((s, d), F32),
        scratch_shapes=[pltpu.VMEM((heads, dk, dv), F32), pltpu.VMEM((heads, cb, cb), F32)],
        compiler_params=_cparams(1),
        name="ret_outproj",
    )(log_gamma, qkvg, x, w_out)


def _ple_kernel(h_ref, p_ref, pn_ref, wgate_ref, wproj_ref, nn_ref, h_out_ref, xn_out_ref, *,
                sub):
    n_sub = h_ref.shape[0] // sub
    rows = [slice(r * sub, (r + 1) * sub) for r in range(n_sub)]

    def normed(r):
        return _rms(h_ref[rows[r], :], pn_ref[...]).astype(BF16)

    def matmuls(r, xn):
        logits = jnp.dot(xn, wgate_ref[...], preferred_element_type=F32)
        emb = jnp.dot(p_ref[rows[r], :].astype(BF16), wproj_ref[...], preferred_element_type=F32)
        return logits, emb

    def finish(r, logits, emb):
        h2 = h_ref[rows[r], :] + _sigmoid(logits) * emb
        h_out_ref[rows[r], :] = h2
        xn_out_ref[rows[r], :] = _rms(h2, nn_ref[...]).astype(BF16)

    xns, mms = {}, {}
    for step in range(n_sub + 2):
        if step < n_sub:
            xns[step] = normed(step)
        if 0 <= step - 1 < n_sub:
            mms[step - 1] = matmuls(step - 1, xns.pop(step - 1))
        if 0 <= step - 2 < n_sub:
            finish(step - 2, *mms.pop(step - 2))


def _ple(h, p, layer, ple_norm, w_gate, w_proj, next_norm, tm=512, sub=256):
    s, d = h.shape
    pd = p.shape[-1]
    row = lambda i: (i, 0)
    resident = lambda shape, idx: pl.BlockSpec(shape, lambda i: idx, pipeline_mode=pl.Buffered(1))
    return pl.pallas_call(
        functools.partial(_ple_kernel, sub=sub),
        grid=(s // tm,),
        in_specs=[pl.BlockSpec((tm, d), row),
                  pl.BlockSpec((None, None, tm, pd), lambda i: (layer, 0, i, 0)),
                  resident((None, 1, d), (layer, 0, 0)),
                  resident((None, d, d), (layer, 0, 0)),
                  resident((None, pd, d), (layer, 0, 0)),
                  resident((1, d), (0, 0))],
        out_specs=(pl.BlockSpec((tm, d), row), pl.BlockSpec((tm, d), row)),
        out_shape=(jax.ShapeDtypeStruct((s, d), F32), jax.ShapeDtypeStruct((s, d), BF16)),
        compiler_params=_cparams(1),
        name="ple",
    )(h, p, ple_norm, w_gate, w_proj, next_norm)


def _outproj_ple_kernel(z_ref, h_ref, p_ref, wout_ref, pn_ref, wgate_ref, wproj_ref,
                        nn_ref, *out_refs, final):
    h1 = h_ref[...] + jnp.dot(z_ref[...], wout_ref[...], preferred_element_type=F32)
    xn = _rms(h1, pn_ref[...]).astype(BF16)
    gate = _sigmoid(jnp.dot(xn, wgate_ref[...], preferred_element_type=F32))
    emb = jnp.dot(p_ref[...].astype(BF16), wproj_ref[...], preferred_element_type=F32)
    h2 = h1 + gate * emb
    nxt = _rms(h2, nn_ref[...])
    if final:
        out_refs[0][...] = nxt
    else:
        out_refs[0][...] = h2
        out_refs[1][...] = nxt.astype(BF16)


def _outproj_ple(z, h, p, layer, w_out, ple_norm, w_gate, w_proj, next_norm, final, tm=256):
    s, d = h.shape[-2:]
    kz = z.shape[1]
    pd = p.shape[-1]
    row = lambda i: (i, 0)
    resident = lambda shape, idx: pl.BlockSpec(shape, lambda i: idx, pipeline_mode=pl.Buffered(1))
    h_spec = pl.BlockSpec((tm, d), row) if h.ndim == 2 else pl.BlockSpec((None, tm, d), lambda i: (0, i, 0))
    if final:
        out_shape = (jax.ShapeDtypeStruct((1, s, d), F32),)
        out_specs = (pl.BlockSpec((None, tm, d), lambda i: (0, i, 0)),)
    else:
        out_shape = (jax.ShapeDtypeStruct((s, d), F32), jax.ShapeDtypeStruct((s, d), BF16))
        out_specs = (pl.BlockSpec((tm, d), row), pl.BlockSpec((tm, d), row))
    return pl.pallas_call(
        functools.partial(_outproj_ple_kernel, final=final),
        grid=(s // tm,),
        in_specs=[pl.BlockSpec((tm, kz), row),
                  h_spec,
                  pl.BlockSpec((None, None, tm, pd), lambda i: (layer, 0, i, 0)),
                  resident((kz, d), (0, 0)),
                  resident((None, 1, d), (layer, 0, 0)),
                  resident((None, d, d), (layer, 0, 0)),
                  resident((None, pd, d), (layer, 0, 0)),
                  resident((1, d), (0, 0))],
        out_specs=out_specs,
        out_shape=out_shape,
        compiler_params=_cparams(1),
        name="outproj_ple_final" if final else "outproj_ple",
    )(z, h, p, w_out, ple_norm, w_gate, w_proj, next_norm)


def _pool_kernel(xn_ref, wu_ref, wg_ref, wgrp_ref, scale_ref, o_ref,
                 ubuf_ref, s2_ref, s4_ref, s8_ref):
    grp = pl.program_id(0)
    i = pl.program_id(1)
    tm = xn_ref.shape[0]
    hl = POOL_HALO
    end = hl + tm

    @pl.when(i == 0)
    def _():
        ubuf_ref[0:hl, :] = jnp.zeros((hl, ubuf_ref.shape[1]), F32)

    xn = xn_ref[...]
    u = jnp.dot(xn, wu_ref[...], preferred_element_type=F32)
    gate = jnp.dot(xn, wg_ref[...], preferred_element_type=F32)

    ubuf_ref[hl:end, :] = u
    s2_ref[8:end, :] = ubuf_ref[8:end, :] + ubuf_ref[7:end - 1, :]
    s4_ref[16:end, :] = s2_ref[16:end, :] + s2_ref[14:end - 2, :]
    s8_ref[24:end, :] = s4_ref[24:end, :] + s4_ref[20:end - 4, :]
    s16 = s8_ref[hl:end, :] + s8_ref[hl - 8:end - 8, :]
    wsum = jnp.where(grp == 0, s2_ref[hl:end, :],
                     jnp.where(grp == 1, s4_ref[hl:end, :],
                               jnp.where(grp == 2, s8_ref[hl:end, :], s16)))
    ubuf_ref[0:hl, :] = ubuf_ref[tm:end, :]

    window = jnp.left_shift(2, grp)
    tok = i * tm + lax.broadcasted_iota(jnp.int32, (tm, 1), 0)
    inv_cnt = 1.0 / jnp.minimum(tok + 1, window).astype(F32)
    dev = (wsum * inv_cnt - u).astype(BF16)
    mixed = jnp.dot(dev, wgrp_ref[...], preferred_element_type=F32) * scale_ref[...]
    o_ref[...] = (mixed * _silu(gate)).astype(o_ref.dtype)


def _pool_mixer(xn, w_in, w_group, scale, tm=512):
    s, d = xn.shape
    n_groups, gd, _ = w_group.shape
    width = n_groups * gd
    assert tuple(2 << g for g in range(n_groups)) == POOL_WINDOWS
    halo_buf = pltpu.VMEM((POOL_HALO + tm, gd), F32)
    return pl.pallas_call(
        _pool_kernel,
        grid=(n_groups, s // tm),
        in_specs=[pl.BlockSpec((tm, d), lambda g, i: (i, 0)),
                  pl.BlockSpec((d, gd), lambda g, i: (0, g)),
                  pl.BlockSpec((d, gd), lambda g, i: (0, n_groups + g)),
                  pl.BlockSpec((None, gd, gd), lambda g, i: (g, 0, 0)),
                  pl.BlockSpec((1, gd), lambda g, i: (0, g))],
        out_specs=pl.BlockSpec((tm, gd), lambda g, i: (i, g)),
        out_shape=jax.ShapeDtypeStruct((s, width), BF16),
        scratch_shapes=[halo_buf, halo_buf, halo_buf, halo_buf],
        compiler_params=_cparams(2),
        name="pool_mixer",
    )(xn, w_in, w_in, w_group, scale)


def _rotary_tables(seq, dk, tm):
    freq = ROPE_BASE ** (-jnp.linspace(0.0, 1.0, dk // 2, dtype=F32))
    ang_r = jnp.arange(tm, dtype=F32)[:, None] * freq[None, :]
    ang_t = (jnp.arange(seq // tm, dtype=F32) * tm)[:, None, None] * freq[None, None, :]
    return jnp.cos(ang_r), jnp.sin(ang_r), jnp.cos(ang_t), jnp.sin(ang_t)


def kernel(x, p, ret_norm, ret_w_in, ret_w_out, pool_norm, pool_w_in, pool_w_group, pool_scale,
           pool_w_out, ple_norm, ple_w_gate, ple_w_proj, final_norm):
    b, s, d = x.shape
    assert b == 1
    heads = RET_HEADS
    dk = d // heads
    dv = ret_w_out.shape[1] // heads
    assert ret_w_in.shape[2] == 2 * heads * dk + 2 * heads * dv
    tm_inproj = 1024

    tabs = _rotary_tables(s, dk, tm_inproj)
    log_gamma = jnp.log1p(-(2.0 ** (-5.0 - jnp.arange(heads, dtype=F32))))
    ple_norm3 = ple_norm.reshape(ple_norm.shape[0], 1, d)
    n_layers, pd, _ = ple_w_proj.shape
    n_groups, gd = pool_w_group.shape[1:3]
    pool_width = n_groups * gd

    side = [(ret_w_out.reshape(heads * dv, d), 64),
            (pool_w_in.reshape(d, 2 * pool_width), 32),
            (pool_w_group.reshape(pool_width, gd), 64),
            (pool_w_out.reshape(pool_width, d), 64),
            (ple_w_gate.reshape(n_layers * d, d), 64),
            (ple_w_proj.reshape(n_layers * pd, d), 32)]

    xn0 = _rmsnorm(x, ret_norm)
    qkvg, (w_ret_out, w_pool_in, w_pool_group, w_pool_out, w_gate, w_proj) = _ret_inproj(
        xn0, ret_w_in, tabs, log_gamma, side, heads, heads * dk, heads * dv, dk ** -0.5, tm_inproj)
    w_gate = w_gate.reshape(n_layers, d, d)
    w_proj = w_proj.reshape(n_layers, pd, d)
    h1 = _ret_outproj(qkvg, x, w_ret_out, log_gamma, heads, dk, dv)
    h2, xn2 = _ple(h1, p, 0, ple_norm3, w_gate, w_proj, pool_norm)

    z = _pool_mixer(xn2, w_pool_in, w_pool_group.reshape(n_groups, gd, gd), pool_scale)
    (out,) = _outproj_ple(z, h2, p, 1, w_pool_out, ple_norm3, w_gate, w_proj,
                          final_norm.reshape(1, d), final=True)
    return out
```

```python
import functools

import jax
import jax.numpy as jnp
from jax import lax
from jax.experimental import pallas as pl
from jax.experimental.pallas import tpu as pltpu

F32 = jnp.float32
BF16 = jnp.bfloat16

RET_HEADS = 8
ROPE_BASE = 10000.0
POOL_WINDOWS = (2, 4, 8, 16)
NORM_EPS = 1e-6
GN_EPS = 1e-5

LANES = 128
SUBLANES = 8
MXU_WIDTH = 256
POOL_HALO = 32
RET_BLOCK = 256
VMEM_LIMIT = 56 * 1024 * 1024


def _cparams(n_axes):
    return pltpu.CompilerParams(
        dimension_semantics=("arbitrary",) * n_axes,
        vmem_limit_bytes=VMEM_LIMIT,
    )


def _rms(x, gain):
    ms = jnp.mean(x * x, axis=-1, keepdims=True)
    return x * lax.rsqrt(ms + NORM_EPS) * gain


def _sigmoid(x):
    return 0.5 * jnp.tanh(0.5 * x) + 0.5


def _silu(x):
    return x * _sigmoid(x)


def _rmsnorm_kernel(x_ref, g_ref, o_ref):
    o_ref[...] = _rms(x_ref[...], g_ref[...]).astype(o_ref.dtype)


def _rmsnorm(x, gain, tm=1024):
    _, s, d = x.shape
    return pl.pallas_call(
        _rmsnorm_kernel,
        grid=(s // tm,),
        in_specs=[pl.BlockSpec((None, tm, d), lambda i: (0, i, 0)),
                  pl.BlockSpec((1, d), lambda i: (0, 0))],
        out_specs=pl.BlockSpec((tm, d), lambda i: (i, 0)),
        out_shape=jax.ShapeDtypeStruct((s, d), BF16),
        compiler_params=_cparams(1),
        name="rmsnorm",
    )(x, gain)


def _ret_inproj_kernel(lg_ref, xn_ref, w_ref, cr_ref, sr_ref, ct_ref, st_ref, *rest,
                       n_side, heads, n_q_tiles, n_rot_tiles, n_plain_tiles, k_scale, cb):
    side_in = rest[:n_side]
    o_ref = rest[n_side]
    side_out = rest[n_side + 1:2 * n_side + 1]
    wbf_ref, cs_ref, sn_ref = rest[2 * n_side + 1:]
    n = pl.program_id(0)
    i = pl.program_id(1)
    tm, tn = o_ref.shape
    heads_per_tile = tn // MXU_WIDTH

    @pl.when((i == 0) & (n >= n_rot_tiles))
    def _():
        wbf_ref[...] = w_ref[...].astype(BF16)

    @pl.when((i == 0) & (n < n_rot_tiles))
    def _():
        src = lax.broadcasted_iota(jnp.int32, (MXU_WIDTH, MXU_WIDTH), 0)
        dst = lax.broadcasted_iota(jnp.int32, (MXU_WIDTH, MXU_WIDTH), 1)
        half = MXU_WIDTH // 2
        wanted = jnp.where(dst < half, 2 * dst, 2 * (dst - half) + 1)
        perm = (src == wanted).astype(BF16)
        for c in range(heads_per_tile):
            cols = slice(c * MXU_WIDTH, (c + 1) * MXU_WIDTH)
            wbf_ref[:, cols] = jnp.dot(w_ref[:, cols].astype(BF16), perm,
                                       preferred_element_type=F32).astype(BF16)

    def run(epilogue):
        for src, dst in zip(side_in, side_out):
            dst[...] = src[...].astype(BF16)
        for c in range(heads_per_tile):
            cols = slice(c * MXU_WIDTH, (c + 1) * MXU_WIDTH)
            acc = jnp.dot(xn_ref[...], wbf_ref[:, cols], preferred_element_type=F32)
            o_ref[:, cols] = epilogue(acc, c).astype(o_ref.dtype)

    @pl.when(n < n_rot_tiles)
    def _():
        ct, st = ct_ref[...], st_ref[...]
        cr, sr = cr_ref[...], sr_ref[...]
        cs_ref[...] = ct * cr - st * sr
        sn_ref[...] = st * cr + ct * sr
        is_k = n >= n_q_tiles
        row = lax.broadcasted_iota(jnp.int32, (tm, LANES), 0)
        j = (row & (cb - 1)).astype(F32)
        expo = jnp.where(is_k, cb - 1.0 - j, j + 1.0)
        scale = jnp.where(is_k, k_scale, 1.0).astype(F32)

        def rotate(acc, c):
            head = (n * heads_per_tile + c) % heads
            decay = jnp.exp(expo * lg_ref[head]) * scale
            cd = cs_ref[...] * decay
            sd = sn_ref[...] * decay
            xe, xo = acc[:, :LANES], acc[:, LANES:]
            return jnp.concatenate([xe * cd - xo * sd, xo * cd + xe * sd], axis=1)

        run(rotate)

    @pl.when((n >= n_rot_tiles) & (n < n_plain_tiles))
    def _():
        run(lambda acc, c: acc)

    @pl.when(n >= n_plain_tiles)
    def _():
        run(lambda acc, c: _silu(acc))


def _ret_inproj(xn, w_in, tabs, log_gamma, side_weights, heads, qk_width, v_width, k_scale, tm,
                tn=1024, cb=RET_BLOCK):
    s, d = xn.shape
    n_out = w_in.shape[2]
    cr, sr, ct, st = tabs
    head_w = cr.shape[1]
    assert 2 * head_w == MXU_WIDTH == 2 * LANES and cr.shape[0] == tm
    assert tm % cb == 0 and cb & (cb - 1) == 0
    n_i = s // tm
    n_rot_tiles = 2 * qk_width // tn
    kern = functools.partial(_ret_inproj_kernel, n_side=len(side_weights), heads=heads,
                             n_q_tiles=qk_width // tn, n_rot_tiles=n_rot_tiles,
                             n_plain_tiles=n_rot_tiles + v_width // tn, k_scale=k_scale, cb=cb)
    side_specs, side_shapes = [], []
    side_in_specs = []
    for arr, chunk_rows, (col_blk, n_col_blks) in side_weights:
        rows, cols = arr.shape[0], arr.shape[1] // n_col_blks
        n_chunks = rows // chunk_rows
        assert n_chunks * chunk_rows == rows and n_chunks <= (n_out // tn) * n_i
        in_map = lambda n, i, last=n_chunks - 1, cb_=col_blk: (jnp.minimum(n * n_i + i, last), cb_)
        out_map = lambda n, i, last=n_chunks - 1: (jnp.minimum(n * n_i + i, last), 0)
        side_in_specs.append(pl.BlockSpec((chunk_rows, cols), in_map))
        side_specs.append(pl.BlockSpec((chunk_rows, cols), out_map))
        side_shapes.append(jax.ShapeDtypeStruct((rows, cols), BF16))
    outs = pl.pallas_call(
        kern,
        grid=(n_out // tn, n_i),
        in_specs=[pl.BlockSpec(memory_space=pltpu.SMEM),
                  pl.BlockSpec((tm, d), lambda n, i: (i, 0)),
                  pl.BlockSpec((None, d, tn), lambda n, i: (0, 0, n)),
                  pl.BlockSpec((tm, head_w), lambda n, i: (0, 0)),
                  pl.BlockSpec((tm, head_w), lambda n, i: (0, 0)),
                  pl.BlockSpec((None, 1, head_w), lambda n, i: (i, 0, 0)),
                  pl.BlockSpec((None, 1, head_w), lambda n, i: (i, 0, 0))] + side_in_specs,
        out_specs=[pl.BlockSpec((tm, tn), lambda n, i: (i, n))] + side_specs,
        out_shape=[jax.ShapeDtypeStruct((s, n_out), BF16)] + side_shapes,
        scratch_shapes=[pltpu.VMEM((d, tn), BF16),
                        pltpu.VMEM((tm, head_w), F32),
                        pltpu.VMEM((tm, head_w), F32)],
        compiler_params=_cparams(2),
        name="ret_inproj",
    )(log_gamma, xn, w_in, cr, sr, ct, st, *[sw[0] for sw in side_weights])
    return outs[0], outs[1:]


def _ret_outproj_kernel(lg_ref, qkvg_ref, x_ref, wout_ref, o_ref, state_ref, dmat_ref, *,
                        heads, dk, dv):
    t = pl.program_id(0)
    cb = qkvg_ref.shape[0]

    @pl.when(t == 0)
    def _():
        state_ref[...] = jnp.zeros_like(state_ref)
        r = lax.broadcasted_iota(jnp.int32, (cb, cb), 0)
        c = lax.broadcasted_iota(jnp.int32, (cb, cb), 1)
        for h in range(heads):
            undo = jnp.exp(jnp.full((cb, cb), -cb, F32) * lg_ref[h])
            dmat_ref[h] = jnp.where(r >= c, undo, 0.0)

    k0, v0, g0 = heads * dk, 2 * heads * dk, 2 * heads * dk + heads * dv

    def mix(h):
        qd = qkvg_ref[:, h * dk:(h + 1) * dk]
        kd = qkvg_ref[:, k0 + h * dk:k0 + (h + 1) * dk]
        v = qkvg_ref[:, v0 + h * dv:v0 + (h + 1) * dv]
        block_decay = jnp.exp(jnp.full((1, 1), cb, F32) * lg_ref[h])
        scores = lax.dot_general(qd, kd, (((1,), (1,)), ((), ())), preferred_element_type=F32)
        p = (scores * dmat_ref[h]).astype(BF16)
        state = state_ref[h]
        y = (jnp.dot(p, v, preferred_element_type=F32)
             + jnp.dot(qd, state.astype(BF16), preferred_element_type=F32))
        state_ref[h] = state * block_decay + lax.dot_general(
            kd, v, (((0,), (0,)), ((), ())), preferred_element_type=F32)
        return y

    def norm_gate(h, y):
        gate = qkvg_ref[:, g0 + h * dv:g0 + (h + 1) * dv]
        mu = jnp.mean(y, axis=-1, keepdims=True)
        yc = y - mu
        var = jnp.mean(yc * yc, axis=-1, keepdims=True)
        return (yc * lax.rsqrt(var + GN_EPS)).astype(BF16) * gate

    def project(h, yg, acc):
        return acc + jnp.dot(yg, wout_ref[h * dv:(h + 1) * dv, :], preferred_element_type=F32)

    acc = x_ref[...]
    ys, ygs = {}, {}
    for step in range(heads + 2):
        if step < heads:
            ys[step] = mix(step)
        if 0 <= step - 1 < heads:
            ygs[step - 1] = norm_gate(step - 1, ys.pop(step - 1))
        if 0 <= step - 2 < heads:
            acc = project(step - 2, ygs.pop(step - 2), acc)
    o_ref[...] = acc


def _ret_outproj(qkvg, x, w_out, log_gamma, heads, dk, dv, cb=RET_BLOCK):
    s, n_qkvg = qkvg.shape
    d = w_out.shape[1]
    return pl.pallas_call(
        functools.partial(_ret_outproj_kernel, heads=heads, dk=dk, dv=dv),
        grid=(s // cb,),
        in_specs=[pl.BlockSpec(memory_space=pltpu.SMEM),
                  pl.BlockSpec((cb, n_qkvg), lambda t: (t, 0)),
                  pl.BlockSpec((None, cb, d), lambda t: (0, t, 0)),
                  pl.BlockSpec((heads * dv, d), lambda t: (0, 0), pipeline_mode=pl.Buffered(1))],
        out_specs=pl.BlockSpec((cb, d), lambda t: (t, 0)),
        out_shape=jax.ShapeDtypeStruct((s, d), F32),
        scratch_shapes=[pltpu.VMEM((heads, dk, dv), F32), pltpu.VMEM((heads, cb, cb), F32)],
        compiler_params=_cparams(1),
        name="ret_outproj",
    )(log_gamma, qkvg, x, w_out)


def _ple_kernel(h_ref, p_ref, pn_ref, wgate_ref, wproj_ref, nn_ref, h_out_ref, xn_out_ref, *,
                sub):
    n_sub = h_ref.shape[0] // sub
    rows = [slice(r * sub, (r + 1) * sub) for r in range(n_sub)]

    def normed(r):
        return _rms(h_ref[rows[r], :], pn_ref[...]).astype(BF16)

    def matmuls(r, xn):
        logits = jnp.dot(xn, wgate_ref[...], preferred_element_type=F32)
        emb = jnp.dot(p_ref[rows[r], :].astype(BF16), wproj_ref[...], preferred_element_type=F32)
        return logits, emb

    def finish(r, logits, emb):
        h2 = h_ref[rows[r], :] + _sigmoid(logits) * emb
        h_out_ref[rows[r], :] = h2
        xn_out_ref[rows[r], :] = _rms(h2, nn_ref[...]).astype(BF16)

    xns, mms = {}, {}
    for step in range(n_sub + 2):
        if step < n_sub:
            xns[step] = normed(step)
        if 0 <= step - 1 < n_sub:
            mms[step - 1] = matmuls(step - 1, xns.pop(step - 1))
        if 0 <= step - 2 < n_sub:
            finish(step - 2, *mms.pop(step - 2))


def _ple(h, p, layer, ple_norm, w_gate, w_proj, next_norm, tm=512, sub=256):
    s, d = h.shape
    pd = p.shape[-1]
    row = lambda i: (i, 0)
    resident = lambda shape, idx: pl.BlockSpec(shape, lambda i: idx, pipeline_mode=pl.Buffered(1))
    return pl.pallas_call(
        functools.partial(_ple_kernel, sub=sub),
        grid=(s // tm,),
        in_specs=[pl.BlockSpec((tm, d), row),
                  pl.BlockSpec((None, None, tm, pd), lambda i: (layer, 0, i, 0)),
                  resident((None, 1, d), (layer, 0, 0)),
                  resident((None, d, d), (layer, 0, 0)),
                  resident((None, pd, d), (layer, 0, 0)),
                  resident((1, d), (0, 0))],
        out_specs=(pl.BlockSpec((tm, d), row), pl.BlockSpec((tm, d), row)),
        out_shape=(jax.ShapeDtypeStruct((s, d), F32), jax.ShapeDtypeStruct((s, d), BF16)),
        compiler_params=_cparams(1),
        name="ple",
    )(h, p, ple_norm, w_gate, w_proj, next_norm)


def _outproj_ple_final_kernel(z_ref, h_ref, p_ref, wout_ref, pn_ref, wgate_ref, wproj_ref,
                              fn_ref, o_ref):
    h1 = h_ref[...] + jnp.dot(z_ref[...], wout_ref[...], preferred_element_type=F32)
    xn = _rms(h1, pn_ref[...]).astype(BF16)
    gate = _sigmoid(jnp.dot(xn, wgate_ref[...], preferred_element_type=F32))
    emb = jnp.dot(p_ref[...].astype(BF16), wproj_ref[...], preferred_element_type=F32)
    o_ref[...] = _rms(h1 + gate * emb, fn_ref[...])


def _outproj_ple_final(z, h, p, layer, w_out, ple_norm, w_gate, w_proj, final_norm, tm=256):
    s, d = h.shape
    kz = z.shape[1]
    pd = p.shape[-1]
    row = lambda i: (i, 0)
    resident = lambda shape, idx: pl.BlockSpec(shape, lambda i: idx, pipeline_mode=pl.Buffered(1))
    return pl.pallas_call(
        _outproj_ple_final_kernel,
        grid=(s // tm,),
        in_specs=[pl.BlockSpec((tm, kz), row),
                  pl.BlockSpec((tm, d), row),
                  pl.BlockSpec((None, None, tm, pd), lambda i: (layer, 0, i, 0)),
                  resident((kz, d), (0, 0)),
                  resident((None, 1, d), (layer, 0, 0)),
                  resident((None, d, d), (layer, 0, 0)),
                  resident((None, pd, d), (layer, 0, 0)),
                  resident((1, d), (0, 0))],
        out_specs=pl.BlockSpec((None, tm, d), lambda i: (0, i, 0)),
        out_shape=jax.ShapeDtypeStruct((1, s, d), F32),
        compiler_params=_cparams(1),
        name="outproj_ple_final",
    )(z, h, p, w_out, ple_norm, w_gate, w_proj, final_norm)


def _pool_wfuse_kernel(wu_ref, wgrp_ref, o_ref):
    o_ref[...] = jnp.dot(wu_ref[...].astype(BF16), wgrp_ref[...].astype(BF16),
                         preferred_element_type=F32).astype(o_ref.dtype)


def _pool_wfuse(w_in, w_group):
    _, d, _ = w_in.shape
    _, n_groups, gd, _ = w_group.shape
    return pl.pallas_call(
        _pool_wfuse_kernel,
        grid=(n_groups,),
        in_specs=[pl.BlockSpec((None, d, gd), lambda g: (0, 0, g)),
                  pl.BlockSpec((None, None, gd, gd), lambda g: (0, g, 0, 0))],
        out_specs=pl.BlockSpec((d, gd), lambda g: (0, g)),
        out_shape=jax.ShapeDtypeStruct((d, n_groups * gd), BF16),
        compiler_params=_cparams(1),
        name="pool_wfuse",
    )(w_in, w_group)


def _pool_kernel(xn_ref, wf_ref, wg_ref, scale_ref, o_ref, xbuf_ref, s2_ref, s4_ref, s8_ref, *,
                 windows):
    i = pl.program_id(0)
    tm = xn_ref.shape[0]
    gd = o_ref.shape[1] // len(windows)
    hl = POOL_HALO
    end = hl + tm

    @pl.when(i == 0)
    def _():
        xbuf_ref[0:hl, :] = jnp.zeros((hl, xbuf_ref.shape[1]), F32)

    xn = xn_ref[...]
    x32 = xn.astype(F32)
    xbuf_ref[hl:end, :] = x32
    s2_ref[8:end, :] = xbuf_ref[8:end, :] + xbuf_ref[7:end - 1, :]
    s4_ref[16:end, :] = s2_ref[16:end, :] + s2_ref[14:end - 2, :]
    s8_ref[24:end, :] = s4_ref[24:end, :] + s4_ref[20:end - 4, :]
    wsums = {2: s2_ref[hl:end, :], 4: s4_ref[hl:end, :], 8: s8_ref[hl:end, :],
             16: s8_ref[hl:end, :] + s8_ref[hl - 8:end - 8, :]}
    xbuf_ref[0:hl, :] = xbuf_ref[tm:end, :]

    tok = i * tm + lax.broadcasted_iota(jnp.int32, (tm, 1), 0)
    for g, w in enumerate(windows):
        cols = slice(g * gd, (g + 1) * gd)
        inv_cnt = 1.0 / jnp.minimum(tok + 1, w).astype(F32)
        dev = (wsums[w] * inv_cnt - x32).astype(BF16)
        mixed = jnp.dot(dev, wf_ref[:, cols], preferred_element_type=F32) * scale_ref[:, cols]
        gate = jnp.dot(xn, wg_ref[:, cols], preferred_element_type=F32)
        o_ref[:, cols] = (mixed * _silu(gate)).astype(o_ref.dtype)


def _pool_mixer(xn, w_fused, w_gate, scale, tm=256):
    s, d = xn.shape
    width = w_fused.shape[1]
    assert set(POOL_WINDOWS) == {2, 4, 8, 16} and POOL_HALO >= 2 * max(POOL_WINDOWS)
    halo_buf = pltpu.VMEM((POOL_HALO + tm, d), F32)
    resident = lambda shape: pl.BlockSpec(shape, lambda i: (0, 0), pipeline_mode=pl.Buffered(1))
    return pl.pallas_call(
        functools.partial(_pool_kernel, windows=POOL_WINDOWS),
        grid=(s // tm,),
        in_specs=[pl.BlockSpec((tm, d), lambda i: (i, 0)),
                  resident((d, width)),
                  resident((d, width)),
                  resident((1, width))],
        out_specs=pl.BlockSpec((tm, width), lambda i: (i, 0)),
        out_shape=jax.ShapeDtypeStruct((s, width), BF16),
        scratch_shapes=[halo_buf, halo_buf, halo_buf, halo_buf],
        compiler_params=_cparams(1),
        name="pool_mixer",
    )(xn, w_fused, w_gate, scale)


def _rotary_tables(seq, dk, tm):
    freq = ROPE_BASE ** (-jnp.linspace(0.0, 1.0, dk // 2, dtype=F32))
    ang_r = jnp.arange(tm, dtype=F32)[:, None] * freq[None, :]
    ang_t = (jnp.arange(seq // tm, dtype=F32) * tm)[:, None, None] * freq[None, None, :]
    return jnp.cos(ang_r), jnp.sin(ang_r), jnp.cos(ang_t), jnp.sin(ang_t)


def kernel(x, p, ret_norm, ret_w_in, ret_w_out, pool_norm, pool_w_in, pool_w_group, pool_scale,
           pool_w_out, ple_norm, ple_w_gate, ple_w_proj, final_norm):
    b, s, d = x.shape
    assert b == 1
    heads = RET_HEADS
    dk = d // heads
    dv = ret_w_out.shape[1] // heads
    assert ret_w_in.shape[2] == 2 * heads * dk + 2 * heads * dv
    tm_inproj = 1024

    tabs = _rotary_tables(s, dk, tm_inproj)
    log_gamma = jnp.log1p(-(2.0 ** (-5.0 - jnp.arange(heads, dtype=F32))))
    ple_norm3 = ple_norm.reshape(ple_norm.shape[0], 1, d)
    n_layers, pd, _ = ple_w_proj.shape
    n_groups, gd = pool_w_group.shape[1:3]
    pool_width = n_groups * gd

    whole = (0, 1)
    side = [(ret_w_out.reshape(heads * dv, d), 64, whole),
            (pool_w_in.reshape(d, 2 * pool_width), 32, (1, 2)),
            (pool_w_out.reshape(pool_width, d), 64, whole),
            (ple_w_gate.reshape(n_layers * d, d), 64, whole),
            (ple_w_proj.reshape(n_layers * pd, d), 32, whole)]

    xn0 = _rmsnorm(x, ret_norm)
    qkvg, (w_ret_out, w_pool_gate, w_pool_out, w_gate, w_proj) = _ret_inproj(
        xn0, ret_w_in, tabs, log_gamma, side, heads, heads * dk, heads * dv, dk ** -0.5, tm_inproj)
    w_gate = w_gate.reshape(n_layers, d, d)
    w_proj = w_proj.reshape(n_layers, pd, d)
    h1 = _ret_outproj(qkvg, x, w_ret_out, log_gamma, heads, dk, dv)
    h2, xn2 = _ple(h1, p, 0, ple_norm3, w_gate, w_proj, pool_norm)

    w_pool_fused = _pool_wfuse(pool_w_in, pool_w_group)
    z = _pool_mixer(xn2, w_pool_fused, w_pool_gate, pool_scale)
    return _outproj_ple_final(z, h2, p, 1, w_pool_out, ple_norm3, w_gate, w_proj,
                              final_norm.reshape(1, d))
```

```python
import functools

import jax
import jax.numpy as jnp
from jax import lax
from jax.experimental import pallas as pl
from jax.experimental.pallas import tpu as pltpu

F32 = jnp.float32
BF16 = jnp.bfloat16

RET_HEADS = 8
ROPE_BASE = 10000.0
POOL_WINDOWS = (2, 4, 8, 16)
NORM_EPS = 1e-6
GN_EPS = 1e-5

LANES = 128
SUBLANES = 8
MXU_WIDTH = 256
POOL_HALO = 32
RET_BLOCK = 256
VMEM_LIMIT = 56 * 1024 * 1024


def _cparams(n_axes):
    return pltpu.CompilerParams(
        dimension_semantics=("arbitrary",) * n_axes,
        vmem_limit_bytes=VMEM_LIMIT,
    )


def _rms(x, gain):
    ms = jnp.mean(x * x, axis=-1, keepdims=True)
    return x * lax.rsqrt(ms + NORM_EPS) * gain


def _sigmoid(x):
    return 0.5 * jnp.tanh(0.5 * x) + 0.5


def _silu(x):
    return x * _sigmoid(x)


def _prep_kernel(x_ref, g_ref, wu_ref, wgrp_ref, xn_ref, wf_ref):
    xn_ref[...] = _rms(x_ref[...], g_ref[...]).astype(xn_ref.dtype)
    wf_ref[...] = jnp.dot(wu_ref[...].astype(BF16), wgrp_ref[...].astype(BF16),
                          preferred_element_type=F32).astype(wf_ref.dtype)


def _prep(x, gain, w_in, w_group, tm=512):
    _, s, d = x.shape
    _, n_groups, gd, _ = w_group.shape
    n_steps = s // tm
    per_group = n_steps // n_groups
    assert per_group * n_groups == n_steps and gd % (per_group * MXU_WIDTH) == 0
    wcols = gd // per_group
    return pl.pallas_call(
        _prep_kernel,
        grid=(n_steps,),
        in_specs=[pl.BlockSpec((None, tm, d), lambda i: (0, i, 0)),
                  pl.BlockSpec((1, d), lambda i: (0, 0)),
                  pl.BlockSpec((None, d, gd), lambda i: (0, 0, i // per_group)),
                  pl.BlockSpec((None, None, gd, wcols),
                               lambda i: (0, i // per_group, 0, i % per_group))],
        out_specs=(pl.BlockSpec((tm, d), lambda i: (i, 0)),
                   pl.BlockSpec((d, wcols), lambda i: (0, i))),
        out_shape=(jax.ShapeDtypeStruct((s, d), BF16),
                   jax.ShapeDtypeStruct((d, n_groups * gd), BF16)),
        compiler_params=_cparams(1),
        name="prep",
    )(x, gain, w_in, w_group)


def _ret_inproj_kernel(lg_ref, xn_ref, w_ref, cr_ref, sr_ref, ct_ref, st_ref, *rest,
                       n_side, heads, n_q_tiles, n_rot_tiles, n_plain_tiles, k_scale, cb):
    side_in = rest[:n_side]
    o_ref = rest[n_side]
    side_out = rest[n_side + 1:2 * n_side + 1]
    wbf_ref, cs_ref, sn_ref = rest[2 * n_side + 1:]
    n = pl.program_id(0)
    i = pl.program_id(1)
    tm, tn = o_ref.shape
    heads_per_tile = tn // MXU_WIDTH

    @pl.when((i == 0) & (n >= n_rot_tiles))
    def _():
        wbf_ref[...] = w_ref[...].astype(BF16)

    @pl.when((i == 0) & (n < n_rot_tiles))
    def _():
        src = lax.broadcasted_iota(jnp.int32, (MXU_WIDTH, MXU_WIDTH), 0)
        dst = lax.broadcasted_iota(jnp.int32, (MXU_WIDTH, MXU_WIDTH), 1)
        half = MXU_WIDTH // 2
        wanted = jnp.where(dst < half, 2 * dst, 2 * (dst - half) + 1)
        perm = (src == wanted).astype(BF16)
        for c in range(heads_per_tile):
            cols = slice(c * MXU_WIDTH, (c + 1) * MXU_WIDTH)
            wbf_ref[:, cols] = jnp.dot(w_ref[:, cols].astype(BF16), perm,
                                       preferred_element_type=F32).astype(BF16)

    def run(epilogue):
        for src, dst in zip(side_in, side_out):
            dst[...] = src[...].astype(BF16)
        for c in range(heads_per_tile):
            cols = slice(c * MXU_WIDTH, (c + 1) * MXU_WIDTH)
            acc = jnp.dot(xn_ref[...], wbf_ref[:, cols], preferred_element_type=F32)
            o_ref[:, cols] = epilogue(acc, c).astype(o_ref.dtype)

    @pl.when(n < n_rot_tiles)
    def _():
        ct, st = ct_ref[...], st_ref[...]
        cr, sr = cr_ref[...], sr_ref[...]
        cs_ref[...] = ct * cr - st * sr
        sn_ref[...] = st * cr + ct * sr
        is_k = n >= n_q_tiles
        row = lax.broadcasted_iota(jnp.int32, (tm, LANES), 0)
        j = (row & (cb - 1)).astype(F32)
        expo = jnp.where(is_k, cb - 1.0 - j, j + 1.0)
        scale = jnp.where(is_k, k_scale, 1.0).astype(F32)

        def rotate(acc, c):
            head = (n * heads_per_tile + c) % heads
            decay = jnp.exp(expo * lg_ref[head]) * scale
            cd = cs_ref[...] * decay
            sd = sn_ref[...] * decay
            xe, xo = acc[:, :LANES], acc[:, LANES:]
            return jnp.concatenate([xe * cd - xo * sd, xo * cd + xe * sd], axis=1)

        run(rotate)

    @pl.when((n >= n_rot_tiles) & (n < n_plain_tiles))
    def _():
        run(lambda acc, c: acc)

    @pl.when(n >= n_plain_tiles)
    def _():
        run(lambda acc, c: _silu(acc))


def _ret_inproj(xn, w_in, tabs, log_gamma, side_weights, heads, qk_width, v_width, k_scale, tm,
                tn=1024, cb=RET_BLOCK):
    s, d = xn.shape
    n_out = w_in.shape[2]
    cr, sr, ct, st = tabs
    head_w = cr.shape[1]
    assert 2 * head_w == MXU_WIDTH == 2 * LANES and cr.shape[0] == tm
    assert tm % cb == 0 and cb & (cb - 1) == 0
    n_i = s // tm
    n_rot_tiles = 2 * qk_width // tn
    kern = functools.partial(_ret_inproj_kernel, n_side=len(side_weights), heads=heads,
                             n_q_tiles=qk_width // tn, n_rot_tiles=n_rot_tiles,
                             n_plain_tiles=n_rot_tiles + v_width // tn, k_scale=k_scale, cb=cb)
    side_specs, side_shapes = [], []
    side_in_specs = []
    for arr, chunk_rows, (col_blk, n_col_blks) in side_weights:
        rows, cols = arr.shape[0], arr.shape[1] // n_col_blks
        n_chunks = rows // chunk_rows
        assert n_chunks * chunk_rows == rows and n_chunks <= (n_out // tn) * n_i
        in_map = lambda n, i, last=n_chunks - 1, cb_=col_blk: (jnp.minimum(n * n_i + i, last), cb_)
        out_map = lambda n, i, last=n_chunks - 1: (jnp.minimum(n * n_i + i, last), 0)
        side_in_specs.append(pl.BlockSpec((chunk_rows, cols), in_map))
        side_specs.append(pl.BlockSpec((chunk_rows, cols), out_map))
        side_shapes.append(jax.ShapeDtypeStruct((rows, cols), BF16))
    outs = pl.pallas_call(
        kern,
        grid=(n_out // tn, n_i),
        in_specs=[pl.BlockSpec(memory_space=pltpu.SMEM),
                  pl.BlockSpec((tm, d), lambda n, i: (i, 0)),
                  pl.BlockSpec((None, d, tn), lambda n, i: (0, 0, n)),
                  pl.BlockSpec((tm, head_w), lambda n, i: (0, 0)),
                  pl.BlockSpec((tm, head_w), lambda n, i: (0, 0)),
                  pl.BlockSpec((None, 1, head_w), lambda n, i: (i, 0, 0)),
                  pl.BlockSpec((None, 1, head_w), lambda n, i: (i, 0, 0))] + side_in_specs,
        out_specs=[pl.BlockSpec((tm, tn), lambda n, i: (i, n))] + side_specs,
        out_shape=[jax.ShapeDtypeStruct((s, n_out), BF16)] + side_shapes,
        scratch_shapes=[pltpu.VMEM((d, tn), BF16),
                        pltpu.VMEM((tm, head_w), F32),
                        pltpu.VMEM((tm, head_w), F32)],
        compiler_params=_cparams(2),
        name="ret_inproj",
    )(log_gamma, xn, w_in, cr, sr, ct, st, *[sw[0] for sw in side_weights])
    return outs[0], outs[1:]


def _ret_outproj_kernel(lg_ref, qkvg_ref, x_ref, wout_ref, o_ref, state_ref, dmat_ref, *,
                        heads, dk, dv):
    t = pl.program_id(0)
    cb = qkvg_ref.shape[0]

    @pl.when(t == 0)
    def _():
        state_ref[...] = jnp.zeros_like(state_ref)
        r = lax.broadcasted_iota(jnp.int32, (cb, cb), 0)
        c = lax.broadcasted_iota(jnp.int32, (cb, cb), 1)
        for h in range(heads):
            undo = jnp.exp(jnp.full((cb, cb), -cb, F32) * lg_ref[h])
            dmat_ref[h] = jnp.where(r >= c, undo, 0.0)

    k0, v0, g0 = heads * dk, 2 * heads * dk, 2 * heads * dk + heads * dv

    def mix(h):
        qd = qkvg_ref[:, h * dk:(h + 1) * dk]
        kd = qkvg_ref[:, k0 + h * dk:k0 + (h + 1) * dk]
        v = qkvg_ref[:, v0 + h * dv:v0 + (h + 1) * dv]
        block_decay = jnp.exp(jnp.full((1, 1), cb, F32) * lg_ref[h])
        scores = lax.dot_general(qd, kd, (((1,), (1,)), ((), ())), preferred_element_type=F32)
        p = (scores * dmat_ref[h]).astype(BF16)
        state = state_ref[h]
        y = (jnp.dot(p, v, preferred_element_type=F32)
             + jnp.dot(qd, state.astype(BF16), preferred_element_type=F32))
        state_ref[h] = state * block_decay + lax.dot_general(
            kd, v, (((0,), (0,)), ((), ())), preferred_element_type=F32)
        return y

    def norm_gate(h, y):
        gate = qkvg_ref[:, g0 + h * dv:g0 + (h + 1) * dv]
        mu = jnp.mean(y, axis=-1, keepdims=True)
        yc = y - mu
        var = jnp.mean(yc * yc, axis=-1, keepdims=True)
        return (yc * lax.rsqrt(var + GN_EPS)).astype(BF16) * gate

    def project(h, yg, acc):
        return acc + jnp.dot(yg, wout_ref[h * dv:(h + 1) * dv, :], preferred_element_type=F32)

    acc = x_ref[...]
    ys, ygs = {}, {}
    for step in range(heads + 2):
        if step < heads:
            ys[step] = mix(step)
        if 0 <= step - 1 < heads:
            ygs[step - 1] = norm_gate(step - 1, ys.pop(step - 1))
        if 0 <= step - 2 < heads:
            acc = project(step - 2, ygs.pop(step - 2), acc)
    o_ref[...] = acc


def _ret_outproj(qkvg, x, w_out, log_gamma, heads, dk, dv, cb=RET_BLOCK):
    s, n_qkvg = qkvg.shape
    d = w_out.shape[1]
    return pl.pallas_call(
        functools.partial(_ret_outproj_kernel, heads=heads, dk=dk, dv=dv),
        grid=(s // cb,),
        in_specs=[pl.BlockSpec(memory_space=pltpu.SMEM),
                  pl.BlockSpec((cb, n_qkvg), lambda t: (t, 0)),
                  pl.BlockSpec((None, cb, d), lambda t: (0, t, 0)),
                  pl.BlockSpec((heads * dv, d), lambda t: (0, 0), pipeline_mode=pl.Buffered(1))],
        out_specs=pl.BlockSpec((cb, d), lambda t: (t, 0)),
        out_shape=jax.ShapeDtypeStruct((s, d), F32),
        scratch_shapes=[pltpu.VMEM((heads, dk, dv), F32), pltpu.VMEM((heads, cb, cb), F32)],
        compiler_params=_cparams(1),
        name="ret_outproj",
    )(log_gamma, qkvg, x, w_out)


def _ple_kernel(h_ref, p_ref, pn_ref, wgate_ref, wproj_ref, nn_ref, h_out_ref, xn_out_ref, *,
                sub):
    n_sub = h_ref.shape[0] // sub
    rows = [slice(r * sub, (r + 1) * sub) for r in range(n_sub)]

    def normed(r):
        return _rms(h_ref[rows[r], :], pn_ref[...]).astype(BF16)

    def matmuls(r, xn):
        logits = jnp.dot(xn, wgate_ref[...], preferred_element_type=F32)
        emb = jnp.dot(p_ref[rows[r], :].astype(BF16), wproj_ref[...], preferred_element_type=F32)
        return logits, emb

    def finish(r, logits, emb):
        h2 = h_ref[rows[r], :] + _sigmoid(logits) * emb
        h_out_ref[rows[r], :] = h2
        xn_out_ref[rows[r], :] = _rms(h2, nn_ref[...]).astype(BF16)

    xns, mms = {}, {}
    for step in range(n_sub + 2):
        if step < n_sub:
            xns[step] = normed(step)
        if 0 <= step - 1 < n_sub:
            mms[step - 1] = matmuls(step - 1, xns.pop(step - 1))
        if 0 <= step - 2 < n_sub:
            finish(step - 2, *mms.pop(step - 2))


def _ple(h, p, layer, ple_norm, w_gate, w_proj, next_norm, tm=512, sub=256):
    s, d = h.shape
    pd = p.shape[-1]
    row = lambda i: (i, 0)
    resident = lambda shape, idx: pl.BlockSpec(shape, lambda i: idx, pipeline_mode=pl.Buffered(1))
    return pl.pallas_call(
        functools.partial(_ple_kernel, sub=sub),
        grid=(s // tm,),
        in_specs=[pl.BlockSpec((tm, d), row),
                  pl.BlockSpec((None, None, tm, pd), lambda i: (layer, 0, i, 0)),
                  resident((None, 1, d), (layer, 0, 0)),
                  resident((None, d, d), (layer, 0, 0)),
                  resident((None, pd, d), (layer, 0, 0)),
                  resident((1, d), (0, 0))],
        out_specs=(pl.BlockSpec((tm, d), row), pl.BlockSpec((tm, d), row)),
        out_shape=(jax.ShapeDtypeStruct((s, d), F32), jax.ShapeDtypeStruct((s, d), BF16)),
        compiler_params=_cparams(1),
        name="ple",
    )(h, p, ple_norm, w_gate, w_proj, next_norm)


def _outproj_ple_final_kernel(z_ref, h_ref, p_ref, wout_ref, pn_ref, wgate_ref, wproj_ref,
                              fn_ref, o_ref):
    h1 = h_ref[...] + jnp.dot(z_ref[...], wout_ref[...], preferred_element_type=F32)
    xn = _rms(h1, pn_ref[...]).astype(BF16)
    gate = _sigmoid(jnp.dot(xn, wgate_ref[...], preferred_element_type=F32))
    emb = jnp.dot(p_ref[...].astype(BF16), wproj_ref[...], preferred_element_type=F32)
    o_ref[...] = _rms(h1 + gate * emb, fn_ref[...])


def _outproj_ple_final(z, h, p, layer, w_out, ple_norm, w_gate, w_proj, final_norm, tm=256):
    s, d = h.shape
    kz = z.shape[1]
    pd = p.shape[-1]
    row = lambda i: (i, 0)
    resident = lambda shape, idx: pl.BlockSpec(shape, lambda i: idx, pipeline_mode=pl.Buffered(1))
    return pl.pallas_call(
        _outproj_ple_final_kernel,
        grid=(s // tm,),
        in_specs=[pl.BlockSpec((tm, kz), row),
                  pl.BlockSpec((tm, d), row),
                  pl.BlockSpec((None, None, tm, pd), lambda i: (layer, 0, i, 0)),
                  resident((kz, d), (0, 0)),
                  resident((None, 1, d), (layer, 0, 0)),
                  resident((None, d, d), (layer, 0, 0)),
                  resident((None, pd, d), (layer, 0, 0)),
                  resident((1, d), (0, 0))],
        out_specs=pl.BlockSpec((None, tm, d), lambda i: (0, i, 0)),
        out_shape=jax.ShapeDtypeStruct((1, s, d), F32),
        compiler_params=_cparams(1),
        name="outproj_ple_final",
    )(z, h, p, w_out, ple_norm, w_gate, w_proj, final_norm)


def _pool_kernel(xn_ref, wf_ref, wg_ref, scale_ref, o_ref, xbuf_ref, s2_ref, s4_ref, s8_ref, *,
                 windows):
    i = pl.program_id(0)
    tm = xn_ref.shape[0]
    gd = o_ref.shape[1] // len(windows)
    hl = POOL_HALO
    end = hl + tm

    @pl.when(i == 0)
    def _():
        xbuf_ref[0:hl, :] = jnp.zeros((hl, xbuf_ref.shape[1]), F32)

    xn = xn_ref[...]
    x32 = xn.astype(F32)
    xbuf_ref[hl:end, :] = x32
    s2_ref[8:end, :] = xbuf_ref[8:end, :] + xbuf_ref[7:end - 1, :]
    s4_ref[16:end, :] = s2_ref[16:end, :] + s2_ref[14:end - 2, :]
    s8_ref[24:end, :] = s4_ref[24:end, :] + s4_ref[20:end - 4, :]
    wsums = {2: s2_ref[hl:end, :], 4: s4_ref[hl:end, :], 8: s8_ref[hl:end, :],
             16: s8_ref[hl:end, :] + s8_ref[hl - 8:end - 8, :]}
    xbuf_ref[0:hl, :] = xbuf_ref[tm:end, :]

    tok = i * tm + lax.broadcasted_iota(jnp.int32, (tm, 1), 0)
    for g, w in enumerate(windows):
        cols = slice(g * gd, (g + 1) * gd)
        inv_cnt = 1.0 / jnp.minimum(tok + 1, w).astype(F32)
        dev = (wsums[w] * inv_cnt - x32).astype(BF16)
        mixed = jnp.dot(dev, wf_ref[:, cols], preferred_element_type=F32) * scale_ref[:, cols]
        gate = jnp.dot(xn, wg_ref[:, cols], preferred_element_type=F32)
        o_ref[:, cols] = (mixed * _silu(gate)).astype(o_ref.dtype)


def _pool_mixer(xn, w_fused, w_gate, scale, tm=256):
    s, d = xn.shape
    width = w_fused.shape[1]
    assert set(POOL_WINDOWS) == {2, 4, 8, 16} and POOL_HALO >= 2 * max(POOL_WINDOWS)
    halo_buf = pltpu.VMEM((POOL_HALO + tm, d), F32)
    resident = lambda shape: pl.BlockSpec(shape, lambda i: (0, 0), pipeline_mode=pl.Buffered(1))
    return pl.pallas_call(
        functools.partial(_pool_kernel, windows=POOL_WINDOWS),
        grid=(s // tm,),
        in_specs=[pl.BlockSpec((tm, d), lambda i: (i, 0)),
                  resident((d, width)),
                  resident((d, width)),
                  resident((1, width))],
        out_specs=pl.BlockSpec((tm, width), lambda i: (i, 0)),
        out_shape=jax.ShapeDtypeStruct((s, width), BF16),
        scratch_shapes=[halo_buf, halo_buf, halo_buf, halo_buf],
        compiler_params=_cparams(1),
        name="pool_mixer",
    )(xn, w_fused, w_gate, scale)


def _rotary_tables(seq, dk, tm):
    freq = ROPE_BASE ** (-jnp.linspace(0.0, 1.0, dk // 2, dtype=F32))
    ang_r = jnp.arange(tm, dtype=F32)[:, None] * freq[None, :]
    ang_t = (jnp.arange(seq // tm, dtype=F32) * tm)[:, None, None] * freq[None, None, :]
    return jnp.cos(ang_r), jnp.sin(ang_r), jnp.cos(ang_t), jnp.sin(ang_t)


def kernel(x, p, ret_norm, ret_w_in, ret_w_out, pool_norm, pool_w_in, pool_w_group, pool_scale,
           pool_w_out, ple_norm, ple_w_gate, ple_w_proj, final_norm):
    b, s, d = x.shape
    assert b == 1
    heads = RET_HEADS
    dk = d // heads
    dv = ret_w_out.shape[1] // heads
    assert ret_w_in.shape[2] == 2 * heads * dk + 2 * heads * dv
    tm_inproj = 1024

    tabs = _rotary_tables(s, dk, tm_inproj)
    log_gamma = jnp.log1p(-(2.0 ** (-5.0 - jnp.arange(heads, dtype=F32))))
    ple_norm3 = ple_norm.reshape(ple_norm.shape[0], 1, d)
    n_layers, pd, _ = ple_w_proj.shape
    n_groups, gd = pool_w_group.shape[1:3]
    pool_width = n_groups * gd

    whole = (0, 1)
    side = [(ret_w_out.reshape(heads * dv, d), 64, whole),
            (pool_w_in.reshape(d, 2 * pool_width), 32, (1, 2)),
            (pool_w_out.reshape(pool_width, d), 64, whole),
            (ple_w_gate.reshape(n_layers * d, d), 64, whole),
            (ple_w_proj.reshape(n_layers * pd, d), 32, whole)]

    xn0, w_pool_fused = _prep(x, ret_norm, pool_w_in, pool_w_group)
    qkvg, (w_ret_out, w_pool_gate, w_pool_out, w_gate, w_proj) = _ret_inproj(
        xn0, ret_w_in, tabs, log_gamma, side, heads, heads * dk, heads * dv, dk ** -0.5, tm_inproj)
    w_gate = w_gate.reshape(n_layers, d, d)
    w_proj = w_proj.reshape(n_layers, pd, d)
    h1 = _ret_outproj(qkvg, x, w_ret_out, log_gamma, heads, dk, dv)
    h2, xn2 = _ple(h1, p, 0, ple_norm3, w_gate, w_proj, pool_norm)

    z = _pool_mixer(xn2, w_pool_fused, w_pool_gate, pool_scale)
    return _outproj_ple_final(z, h2, p, 1, w_pool_out, ple_norm3, w_gate, w_proj,
                              final_norm.reshape(1, d))
```

```python
import functools

import jax
import jax.numpy as jnp
from jax import lax
from jax.experimental import pallas as pl
from jax.experimental.pallas import tpu as pltpu

F32 = jnp.float32
BF16 = jnp.bfloat16

RET_HEADS = 8
ROPE_BASE = 10000.0
POOL_WINDOWS = (2, 4, 8, 16)
NORM_EPS = 1e-6
GN_EPS = 1e-5

LANES = 128
SUBLANES = 8
MXU_WIDTH = 256
POOL_HALO = 32
RET_BLOCK = 256
VMEM_LIMIT = 56 * 1024 * 1024


def _cparams(n_axes):
    return pltpu.CompilerParams(
        dimension_semantics=("arbitrary",) * n_axes,
        vmem_limit_bytes=VMEM_LIMIT,
    )


def _rms(x, gain):
    ms = jnp.mean(x * x, axis=-1, keepdims=True)
    return x * lax.rsqrt(ms + NORM_EPS) * gain


def _sigmoid(x):
    return 0.5 * jnp.tanh(0.5 * x) + 0.5


def _silu(x):
    return x * _sigmoid(x)


def _ret_inproj_kernel(lg_ref, lhs_ref, *rest, fuse_norm, n_side, tile0, n_tiles, heads,
                       n_q_tiles, n_rot_tiles, n_plain_tiles, k_scale, cb):
    if fuse_norm:
        gain_ref, rest = rest[0], rest[1:]
    w_ref, cr_ref, sr_ref, ct_ref, st_ref = rest[:5]
    rest = rest[5:]
    side_in, rest = rest[:n_side], rest[n_side:]
    o_ref, rest = rest[0], rest[1:]
    if fuse_norm:
        xn_ref, rest = rest[0], rest[1:]
    else:
        xn_ref = lhs_ref
    side_out, rest = rest[:n_side], rest[n_side:]
    wbf_ref, cs_ref, sn_ref = rest
    n = tile0 if n_tiles == 1 else pl.program_id(0) + tile0
    i = pl.program_id(1)
    tm, tn = o_ref.shape
    heads_per_tile = tn // MXU_WIDTH

    def when(cond):
        if isinstance(cond, bool):
            return lambda f: f() if cond else None
        return pl.when(cond)

    @when(n >= n_rot_tiles)
    def _():
        @pl.when(i == 0)
        def _():
            wbf_ref[...] = w_ref[...].astype(BF16)

    @when(n < n_rot_tiles)
    def _():
        @pl.when(i == 0)
        def _():
            src = lax.broadcasted_iota(jnp.int32, (MXU_WIDTH, MXU_WIDTH), 0)
            dst = lax.broadcasted_iota(jnp.int32, (MXU_WIDTH, MXU_WIDTH), 1)
            half = MXU_WIDTH // 2
            wanted = jnp.where(dst < half, 2 * dst, 2 * (dst - half) + 1)
            perm = (src == wanted).astype(BF16)
            for c in range(heads_per_tile):
                cols = slice(c * MXU_WIDTH, (c + 1) * MXU_WIDTH)
                wbf_ref[:, cols] = jnp.dot(w_ref[:, cols].astype(BF16), perm,
                                           preferred_element_type=F32).astype(BF16)

    if fuse_norm:
        xn_ref[...] = _rms(lhs_ref[...], gain_ref[...]).astype(xn_ref.dtype)

    def run(epilogue):
        for src, dst in zip(side_in, side_out):
            dst[...] = src[...].astype(BF16)
        for c in range(heads_per_tile):
            cols = slice(c * MXU_WIDTH, (c + 1) * MXU_WIDTH)
            acc = jnp.dot(xn_ref[...], wbf_ref[:, cols], preferred_element_type=F32)
            o_ref[:, cols] = epilogue(acc, c).astype(o_ref.dtype)

    @when(n < n_rot_tiles)
    def _():
        ct, st = ct_ref[...], st_ref[...]
        cr, sr = cr_ref[...], sr_ref[...]
        cs_ref[...] = ct * cr - st * sr
        sn_ref[...] = st * cr + ct * sr
        is_k = n >= n_q_tiles
        row = lax.broadcasted_iota(jnp.int32, (tm, LANES), 0)
        j = (row & (cb - 1)).astype(F32)
        expo = jnp.where(is_k, cb - 1.0 - j, j + 1.0)
        scale = jnp.where(is_k, k_scale, 1.0).astype(F32)

        def rotate(acc, c):
            head = (n * heads_per_tile + c) % heads
            decay = jnp.exp(expo * lg_ref[head]) * scale
            cd = cs_ref[...] * decay
            sd = sn_ref[...] * decay
            xe, xo = acc[:, :LANES], acc[:, LANES:]
            return jnp.concatenate([xe * cd - xo * sd, xo * cd + xe * sd], axis=1)

        run(rotate)

    @when((n >= n_rot_tiles) & (n < n_plain_tiles))
    def _():
        run(lambda acc, c: acc)

    @when(n >= n_plain_tiles)
    def _():
        run(lambda acc, c: _silu(acc))


def _ret_inproj(lhs, w_in, tabs, log_gamma, side_weights, heads, qk_width, v_width, k_scale, tm,
                tile0, n_tiles, gain=None, tn=1024, cb=RET_BLOCK):
    fuse_norm = gain is not None
    assert not (fuse_norm and (side_weights or n_tiles != 1))
    s, d = lhs.shape[-2:]
    cr, sr, ct, st = tabs
    head_w = cr.shape[1]
    assert 2 * head_w == MXU_WIDTH == 2 * LANES and cr.shape[0] == tm
    assert tm % cb == 0 and cb & (cb - 1) == 0
    n_i = s // tm
    n_rot_tiles = 2 * qk_width // tn
    kern = functools.partial(_ret_inproj_kernel, fuse_norm=fuse_norm, n_side=len(side_weights),
                             tile0=tile0, n_tiles=n_tiles, heads=heads, n_q_tiles=qk_width // tn,
                             n_rot_tiles=n_rot_tiles, n_plain_tiles=n_rot_tiles + v_width // tn,
                             k_scale=k_scale, cb=cb)
    side_in_specs, side_specs, side_shapes = [], [], []
    for arr, chunk_rows, (col_blk, n_col_blks) in side_weights:
        rows, cols = arr.shape[0], arr.shape[1] // n_col_blks
        n_chunks = rows // chunk_rows
        assert n_chunks * chunk_rows == rows and n_chunks <= n_tiles * n_i
        in_map = lambda n, i, last=n_chunks - 1, cb_=col_blk: (jnp.minimum(n * n_i + i, last), cb_)
        out_map = lambda n, i, last=n_chunks - 1: (jnp.minimum(n * n_i + i, last), 0)
        side_in_specs.append(pl.BlockSpec((chunk_rows, cols), in_map))
        side_specs.append(pl.BlockSpec((chunk_rows, cols), out_map))
        side_shapes.append(jax.ShapeDtypeStruct((rows, cols), BF16))
    if fuse_norm:
        lhs_specs = [pl.BlockSpec((None, tm, d), lambda n, i: (0, i, 0)),
                     pl.BlockSpec((1, d), lambda n, i: (0, 0))]
        lhs_args = (lhs, gain)
        extra_specs = [pl.BlockSpec((tm, d), lambda n, i: (i, 0))]
        extra_shapes = [jax.ShapeDtypeStruct((s, d), BF16)]
        w_mode = dict(pipeline_mode=pl.Buffered(1))
    else:
        lhs_specs = [pl.BlockSpec((tm, d), lambda n, i: (i, 0))]
        lhs_args = (lhs,)
        extra_specs, extra_shapes, w_mode = [], [], {}
    const = lambda n, i: (0, 0)
    outs = pl.pallas_call(
        kern,
        grid=(n_tiles, n_i),
        in_specs=[pl.BlockSpec(memory_space=pltpu.SMEM)] + lhs_specs + [
            pl.BlockSpec((None, d, tn), lambda n, i: (0, 0, n + tile0), **w_mode),
            pl.BlockSpec((tm, head_w), const),
            pl.BlockSpec((tm, head_w), const),
            pl.BlockSpec((None, 1, head_w), lambda n, i: (i, 0, 0)),
            pl.BlockSpec((None, 1, head_w), lambda n, i: (i, 0, 0))] + side_in_specs,
        out_specs=[pl.BlockSpec((tm, tn), lambda n, i: (i, n))] + extra_specs + side_specs,
        out_shape=[jax.ShapeDtypeStruct((s, n_tiles * tn), BF16)] + extra_shapes + side_shapes,
        scratch_shapes=[pltpu.VMEM((d, tn), BF16),
                        pltpu.VMEM((tm, head_w), F32),
                        pltpu.VMEM((tm, head_w), F32)],
        compiler_params=_cparams(2),
        name="ret_inproj_first" if fuse_norm else "ret_inproj",
    )(log_gamma, *lhs_args, w_in, cr, sr, ct, st, *[sw[0] for sw in side_weights])
    return outs[0], outs[1:]


def _ret_outproj_kernel(lg_ref, first_ref, rest_ref, x_ref, wout_ref, o_ref, state_ref, dmat_ref,
                        *, heads, dk, dv):
    t = pl.program_id(0)
    cb, n_first = first_ref.shape

    def cols(start, width):
        if start + width <= n_first:
            return first_ref[:, start:start + width]
        assert start >= n_first
        return rest_ref[:, start - n_first:start - n_first + width]

    @pl.when(t == 0)
    def _():
        state_ref[...] = jnp.zeros_like(state_ref)
        r = lax.broadcasted_iota(jnp.int32, (cb, cb), 0)
        c = lax.broadcasted_iota(jnp.int32, (cb, cb), 1)
        for h in range(heads):
            undo = jnp.exp(jnp.full((cb, cb), -cb, F32) * lg_ref[h])
            dmat_ref[h] = jnp.where(r >= c, undo, 0.0)

    k0, v0, g0 = heads * dk, 2 * heads * dk, 2 * heads * dk + heads * dv

    def mix(h):
        qd = cols(h * dk, dk)
        kd = cols(k0 + h * dk, dk)
        v = cols(v0 + h * dv, dv)
        block_decay = jnp.exp(jnp.full((1, 1), cb, F32) * lg_ref[h])
        scores = lax.dot_general(qd, kd, (((1,), (1,)), ((), ())), preferred_element_type=F32)
        p = (scores * dmat_ref[h]).astype(BF16)
        state = state_ref[h]
        y = (jnp.dot(p, v, preferred_element_type=F32)
             + jnp.dot(qd, state.astype(BF16), preferred_element_type=F32))
        state_ref[h] = state * block_decay + lax.dot_general(
            kd, v, (((0,), (0,)), ((), ())), preferred_element_type=F32)
        return y

    def norm_gate(h, y):
        gate = cols(g0 + h * dv, dv)
        mu = jnp.mean(y, axis=-1, keepdims=True)
        yc = y - mu
        var = jnp.mean(yc * yc, axis=-1, keepdims=True)
        return (yc * lax.rsqrt(var + GN_EPS)).astype(BF16) * gate

    def project(h, yg, acc):
        return acc + jnp.dot(yg, wout_ref[h * dv:(h + 1) * dv, :], preferred_element_type=F32)

    acc = x_ref[...]
    ys, ygs = {}, {}
    for step in range(heads + 2):
        if step < heads:
            ys[step] = mix(step)
        if 0 <= step - 1 < heads:
            ygs[step - 1] = norm_gate(step - 1, ys.pop(step - 1))
        if 0 <= step - 2 < heads:
            acc = project(step - 2, ygs.pop(step - 2), acc)
    o_ref[...] = acc


def _ret_outproj(qkvg_first, qkvg_rest, x, w_out, log_gamma, heads, dk, dv, cb=RET_BLOCK):
    s, n_first = qkvg_first.shape
    n_rest = qkvg_rest.shape[1]
    d = w_out.shape[1]
    return pl.pallas_call(
        functools.partial(_ret_outproj_kernel, heads=heads, dk=dk, dv=dv),
        grid=(s // cb,),
        in_specs=[pl.BlockSpec(memory_space=pltpu.SMEM),
                  pl.BlockSpec((cb, n_first), lambda t: (t, 0)),
                  pl.BlockSpec((cb, n_rest), lambda t: (t, 0)),
                  pl.BlockSpec((None, cb, d), lambda t: (0, t, 0)),
                  pl.BlockSpec((heads * dv, d), lambda t: (0, 0), pipeline_mode=pl.Buffered(1))],
        out_specs=pl.BlockSpec((cb, d), lambda t: (t, 0)),
        out_shape=jax.ShapeDtypeStruct((s, d), F32),
        scratch_shapes=[pltpu.VMEM((heads, dk, dv), F32), pltpu.VMEM((heads, cb, cb), F32)],
        compiler_params=_cparams(1),
        name="ret_outproj",
    )(log_gamma, qkvg_first, qkvg_rest, x, w_out)


def _ple_kernel(h_ref, p_ref, pn_ref, wgate_ref, wproj_ref, nn_ref, h_out_ref, xn_out_ref, *,
                sub):
    n_sub = h_ref.shape[0] // sub
    rows = [slice(r * sub, (r + 1) * sub) for r in range(n_sub)]

    def normed(r):
        return _rms(h_ref[rows[r], :], pn_ref[...]).astype(BF16)

    def matmuls(r, xn):
        logits = jnp.dot(xn, wgate_ref[...], preferred_element_type=F32)
        emb = jnp.dot(p_ref[rows[r], :].astype(BF16), wproj_ref[...], preferred_element_type=F32)
        return logits, emb

    def finish(r, logits, emb):
        h2 = h_ref[rows[r], :] + _sigmoid(logits) * emb
        h_out_ref[rows[r], :] = h2
        xn_out_ref[rows[r], :] = _rms(h2, nn_ref[...]).astype(BF16)

    xns, mms = {}, {}
    for step in range(n_sub + 2):
        if step < n_sub:
            xns[step] = normed(step)
        if 0 <= step - 1 < n_sub:
            mms[step - 1] = matmuls(step - 1, xns.pop(step - 1))
        if 0 <= step - 2 < n_sub:
            finish(step - 2, *mms.pop(step - 2))


def _ple(h, p, layer, ple_norm, w_gate, w_proj, next_norm, tm=512, sub=256):
    s, d = h.shape
    pd = p.shape[-1]
    row = lambda i: (i, 0)
    resident = lambda shape, idx: pl.BlockSpec(shape, lambda i: idx, pipeline_mode=pl.Buffered(1))
    return pl.pallas_call(
        functools.partial(_ple_kernel, sub=sub),
        grid=(s // tm,),
        in_specs=[pl.BlockSpec((tm, d), row),
                  pl.BlockSpec((None, None, tm, pd), lambda i: (layer, 0, i, 0)),
                  resident((None, 1, d), (layer, 0, 0)),
                  resident((None, d, d), (layer, 0, 0)),
                  resident((None, pd, d), (layer, 0, 0)),
                  resident((1, d), (0, 0))],
        out_specs=(pl.BlockSpec((tm, d), row), pl.BlockSpec((tm, d), row)),
        out_shape=(jax.ShapeDtypeStruct((s, d), F32), jax.ShapeDtypeStruct((s, d), BF16)),
        compiler_params=_cparams(1),
        name="ple",
    )(h, p, ple_norm, w_gate, w_proj, next_norm)


def _outproj_ple_final_kernel(z_ref, h_ref, p_ref, wout_ref, pn_ref, wgate_ref, wproj_ref,
                              fn_ref, o_ref):
    h1 = h_ref[...] + jnp.dot(z_ref[...], wout_ref[...], preferred_element_type=F32)
    xn = _rms(h1, pn_ref[...]).astype(BF16)
    gate = _sigmoid(jnp.dot(xn, wgate_ref[...], preferred_element_type=F32))
    emb = jnp.dot(p_ref[...].astype(BF16), wproj_ref[...], preferred_element_type=F32)
    o_ref[...] = _rms(h1 + gate * emb, fn_ref[...])


def _outproj_ple_final(z, h, p, layer, w_out, ple_norm, w_gate, w_proj, final_norm, tm=256):
    s, d = h.shape
    kz = z.shape[1]
    pd = p.shape[-1]
    row = lambda i: (i, 0)
    resident = lambda shape, idx: pl.BlockSpec(shape, lambda i: idx, pipeline_mode=pl.Buffered(1))
    return pl.pallas_call(
        _outproj_ple_final_kernel,
        grid=(s // tm,),
        in_specs=[pl.BlockSpec((tm, kz), row),
                  pl.BlockSpec((tm, d), row),
                  pl.BlockSpec((None, None, tm, pd), lambda i: (layer, 0, i, 0)),
                  resident((kz, d), (0, 0)),
                  resident((None, 1, d), (layer, 0, 0)),
                  resident((None, d, d), (layer, 0, 0)),
                  resident((None, pd, d), (layer, 0, 0)),
                  resident((1, d), (0, 0))],
        out_specs=pl.BlockSpec((None, tm, d), lambda i: (0, i, 0)),
        out_shape=jax.ShapeDtypeStruct((1, s, d), F32),
        compiler_params=_cparams(1),
        name="outproj_ple_final",
    )(z, h, p, w_out, ple_norm, w_gate, w_proj, final_norm)


def _pool_wfuse_kernel(wu_ref, wgrp_ref, o_ref):
    o_ref[...] = jnp.dot(wu_ref[...].astype(BF16), wgrp_ref[...].astype(BF16),
                         preferred_element_type=F32).astype(o_ref.dtype)


def _pool_wfuse(w_in, w_group):
    _, d, _ = w_in.shape
    _, n_groups, gd, _ = w_group.shape
    return pl.pallas_call(
        _pool_wfuse_kernel,
        grid=(n_groups,),
        in_specs=[pl.BlockSpec((None, d, gd), lambda g: (0, 0, g)),
                  pl.BlockSpec((None, None, gd, gd), lambda g: (0, g, 0, 0))],
        out_specs=pl.BlockSpec((d, gd), lambda g: (0, g)),
        out_shape=jax.ShapeDtypeStruct((d, n_groups * gd), BF16),
        compiler_params=_cparams(1),
        name="pool_wfuse",
    )(w_in, w_group)


def _pool_kernel(xn_ref, wf_ref, wg_ref, scale_ref, o_ref, xbuf_ref, s2_ref, s4_ref, s8_ref, *,
                 windows):
    i = pl.program_id(0)
    tm = xn_ref.shape[0]
    gd = o_ref.shape[1] // len(windows)
    hl = POOL_HALO
    end = hl + tm

    @pl.when(i == 0)
    def _():
        xbuf_ref[0:hl, :] = jnp.zeros((hl, xbuf_ref.shape[1]), F32)

    xn = xn_ref[...]
    x32 = xn.astype(F32)
    xbuf_ref[hl:end, :] = x32
    s2_ref[8:end, :] = xbuf_ref[8:end, :] + xbuf_ref[7:end - 1, :]
    s4_ref[16:end, :] = s2_ref[16:end, :] + s2_ref[14:end - 2, :]
    s8_ref[24:end, :] = s4_ref[24:end, :] + s4_ref[20:end - 4, :]
    wsums = {2: s2_ref[hl:end, :], 4: s4_ref[hl:end, :], 8: s8_ref[hl:end, :],
             16: s8_ref[hl:end, :] + s8_ref[hl - 8:end - 8, :]}
    xbuf_ref[0:hl, :] = xbuf_ref[tm:end, :]

    tok = i * tm + lax.broadcasted_iota(jnp.int32, (tm, 1), 0)
    for g, w in enumerate(windows):
        cols = slice(g * gd, (g + 1) * gd)
        inv_cnt = 1.0 / jnp.minimum(tok + 1, w).astype(F32)
        dev = (wsums[w] * inv_cnt - x32).astype(BF16)
        mixed = jnp.dot(dev, wf_ref[:, cols], preferred_element_type=F32) * scale_ref[:, cols]
        gate = jnp.dot(xn, wg_ref[:, cols], preferred_element_type=F32)
        o_ref[:, cols] = (mixed * _silu(gate)).astype(o_ref.dtype)


def _pool_mixer(xn, w_fused, w_gate, scale, tm=256):
    s, d = xn.shape
    width = w_fused.shape[1]
    assert set(POOL_WINDOWS) == {2, 4, 8, 16} and POOL_HALO >= 2 * max(POOL_WINDOWS)
    halo_buf = pltpu.VMEM((POOL_HALO + tm, d), F32)
    resident = lambda shape: pl.BlockSpec(shape, lambda i: (0, 0), pipeline_mode=pl.Buffered(1))
    return pl.pallas_call(
        functools.partial(_pool_kernel, windows=POOL_WINDOWS),
        grid=(s // tm,),
        in_specs=[pl.BlockSpec((tm, d), lambda i: (i, 0)),
                  resident((d, width)),
                  resident((d, width)),
                  resident((1, width))],
        out_specs=pl.BlockSpec((tm, width), lambda i: (i, 0)),
        out_shape=jax.ShapeDtypeStruct((s, width), BF16),
        scratch_shapes=[halo_buf, halo_buf, halo_buf, halo_buf],
        compiler_params=_cparams(1),
        name="pool_mixer",
    )(xn, w_fused, w_gate, scale)


def _rotary_tables(seq, dk, tm):
    freq = ROPE_BASE ** (-jnp.linspace(0.0, 1.0, dk // 2, dtype=F32))
    ang_r = jnp.arange(tm, dtype=F32)[:, None] * freq[None, :]
    ang_t = (jnp.arange(seq // tm, dtype=F32) * tm)[:, None, None] * freq[None, None, :]
    return jnp.cos(ang_r), jnp.sin(ang_r), jnp.cos(ang_t), jnp.sin(ang_t)


def kernel(x, p, ret_norm, ret_w_in, ret_w_out, pool_norm, pool_w_in, pool_w_group, pool_scale,
           pool_w_out, ple_norm, ple_w_gate, ple_w_proj, final_norm):
    b, s, d = x.shape
    assert b == 1
    heads = RET_HEADS
    dk = d // heads
    dv = ret_w_out.shape[1] // heads
    assert ret_w_in.shape[2] == 2 * heads * dk + 2 * heads * dv
    tm_inproj = tn_inproj = 1024

    tabs = _rotary_tables(s, dk, tm_inproj)
    log_gamma = jnp.log1p(-(2.0 ** (-5.0 - jnp.arange(heads, dtype=F32))))
    ple_norm3 = ple_norm.reshape(ple_norm.shape[0], 1, d)
    n_layers, pd, _ = ple_w_proj.shape
    n_groups, gd = pool_w_group.shape[1:3]
    pool_width = n_groups * gd

    whole = (0, 1)
    side = [(ret_w_out.reshape(heads * dv, d), 64, whole),
            (pool_w_in.reshape(d, 2 * pool_width), 32, (1, 2)),
            (pool_w_out.reshape(pool_width, d), 64, whole),
            (ple_w_gate.reshape(n_layers * d, d), 64, whole),
            (ple_w_proj.reshape(n_layers * pd, d), 32, whole)]

    n_tiles = ret_w_in.shape[2] // tn_inproj
    widths = (heads, heads * dk, heads * dv, dk ** -0.5, tm_inproj)
    qkvg_first, (xn0,) = _ret_inproj(x, ret_w_in, tabs, log_gamma, [], *widths, 0, 1, gain=ret_norm)
    qkvg_rest, (w_ret_out, w_pool_gate, w_pool_out, w_gate, w_proj) = _ret_inproj(
        xn0, ret_w_in, tabs, log_gamma, side, *widths, 1, n_tiles - 1)
    w_gate = w_gate.reshape(n_layers, d, d)
    w_proj = w_proj.reshape(n_layers, pd, d)
    h1 = _ret_outproj(qkvg_first, qkvg_rest, x, w_ret_out, log_gamma, heads, dk, dv)
    h2, xn2 = _ple(h1, p, 0, ple_norm3, w_gate, w_proj, pool_norm)

    w_pool_fused = _pool_wfuse(pool_w_in, pool_w_group)
    z = _pool_mixer(xn2, w_pool_fused, w_pool_gate, pool_scale)
    return _outproj_ple_final(z, h2, p, 1, w_pool_out, ple_norm3, w_gate, w_proj,
                              final_norm.reshape(1, d))
```

```python
import functools

import jax
import jax.numpy as jnp
from jax import lax
from jax.experimental import pallas as pl
from jax.experimental.pallas import tpu as pltpu

F32 = jnp.float32
BF16 = jnp.bfloat16

RET_HEADS = 8
ROPE_BASE = 10000.0
POOL_WINDOWS = (2, 4, 8, 16)
NORM_EPS = 1e-6
GN_EPS = 1e-5

LANES = 128
SUBLANES = 8
MXU_WIDTH = 256
POOL_HALO = 32
RET_BLOCK = 256
VMEM_LIMIT = 56 * 1024 * 1024


def _cparams(n_axes):
    return pltpu.CompilerParams(
        dimension_semantics=("arbitrary",) * n_axes,
        vmem_limit_bytes=VMEM_LIMIT,
    )


def _rms(x, gain):
    ms = jnp.mean(x * x, axis=-1, keepdims=True)
    return x * lax.rsqrt(ms + NORM_EPS) * gain


def _sigmoid(x):
    return 0.5 * jnp.tanh(0.5 * x) + 0.5


def _silu(x):
    return x * _sigmoid(x)


def _ret_inproj_kernel(lg_ref, lhs_ref, *rest, fuse_norm, n_side, tile0, n_tiles, heads,
                       n_q_tiles, n_rot_tiles, n_plain_tiles, k_scale, cb):
    if fuse_norm:
        gain_ref, rest = rest[0], rest[1:]
    w_ref, cr_ref, sr_ref, ct_ref, st_ref = rest[:5]
    rest = rest[5:]
    side_in, rest = rest[:n_side], rest[n_side:]
    o_ref, rest = rest[0], rest[1:]
    if fuse_norm:
        xn_ref, rest = rest[0], rest[1:]
    else:
        xn_ref = lhs_ref
    side_out, rest = rest[:n_side], rest[n_side:]
    wbf_ref, cs_ref, sn_ref = rest
    n = tile0 if n_tiles == 1 else pl.program_id(0) + tile0
    i = pl.program_id(1)
    tm, tn = o_ref.shape
    heads_per_tile = tn // MXU_WIDTH

    def when(cond):
        if isinstance(cond, bool):
            return lambda f: f() if cond else None
        return pl.when(cond)

    @when(n >= n_rot_tiles)
    def _():
        @pl.when(i == 0)
        def _():
            wbf_ref[...] = w_ref[...].astype(BF16)

    @when(n < n_rot_tiles)
    def _():
        @pl.when(i == 0)
        def _():
            src = lax.broadcasted_iota(jnp.int32, (MXU_WIDTH, MXU_WIDTH), 0)
            dst = lax.broadcasted_iota(jnp.int32, (MXU_WIDTH, MXU_WIDTH), 1)
            half = MXU_WIDTH // 2
            wanted = jnp.where(dst < half, 2 * dst, 2 * (dst - half) + 1)
            perm = (src == wanted).astype(BF16)
            for c in range(heads_per_tile):
                cols = slice(c * MXU_WIDTH, (c + 1) * MXU_WIDTH)
                wbf_ref[:, cols] = jnp.dot(w_ref[:, cols].astype(BF16), perm,
                                           preferred_element_type=F32).astype(BF16)

    if fuse_norm:
        xn_ref[...] = _rms(lhs_ref[...], gain_ref[...]).astype(xn_ref.dtype)

    def run(epilogue):
        for src, dst in zip(side_in, side_out):
            dst[...] = src[...].astype(BF16)
        for c in range(heads_per_tile):
            cols = slice(c * MXU_WIDTH, (c + 1) * MXU_WIDTH)
            acc = jnp.dot(xn_ref[...], wbf_ref[:, cols], preferred_element_type=F32)
            o_ref[:, cols] = epilogue(acc, c).astype(o_ref.dtype)

    @when(n < n_rot_tiles)
    def _():
        ct, st = ct_ref[...], st_ref[...]
        cr, sr = cr_ref[...], sr_ref[...]
        cs_ref[...] = ct * cr - st * sr
        sn_ref[...] = st * cr + ct * sr
        is_k = n >= n_q_tiles
        row = lax.broadcasted_iota(jnp.int32, (tm, LANES), 0)
        j = (row & (cb - 1)).astype(F32)
        expo = jnp.where(is_k, cb - 1.0 - j, j + 1.0)
        scale = jnp.where(is_k, k_scale, 1.0).astype(F32)

        def rotate(acc, c):
            head = (n * heads_per_tile + c) % heads
            decay = jnp.exp(expo * lg_ref[head]) * scale
            cd = cs_ref[...] * decay
            sd = sn_ref[...] * decay
            xe, xo = acc[:, :LANES], acc[:, LANES:]
            return jnp.concatenate([xe * cd - xo * sd, xo * cd + xe * sd], axis=1)

        run(rotate)

    @when((n >= n_rot_tiles) & (n < n_plain_tiles))
    def _():
        run(lambda acc, c: acc)

    @when(n >= n_plain_tiles)
    def _():
        run(lambda acc, c: _silu(acc))


def _ret_inproj(lhs, w_in, tabs, log_gamma, side_weights, heads, qk_width, v_width, k_scale, tm,
                tile0, n_tiles, gain=None, tn=1024, cb=RET_BLOCK):
    fuse_norm = gain is not None
    assert not (fuse_norm and (side_weights or n_tiles != 1))
    s, d = lhs.shape[-2:]
    cr, sr, ct, st = tabs
    head_w = cr.shape[1]
    assert 2 * head_w == MXU_WIDTH == 2 * LANES and cr.shape[0] == tm
    assert tm % cb == 0 and cb & (cb - 1) == 0
    n_i = s // tm
    n_rot_tiles = 2 * qk_width // tn
    kern = functools.partial(_ret_inproj_kernel, fuse_norm=fuse_norm, n_side=len(side_weights),
                             tile0=tile0, n_tiles=n_tiles, heads=heads, n_q_tiles=qk_width // tn,
                             n_rot_tiles=n_rot_tiles, n_plain_tiles=n_rot_tiles + v_width // tn,
                             k_scale=k_scale, cb=cb)
    side_in_specs, side_specs, side_shapes = [], [], []
    for arr, chunk_rows, (col_blk, n_col_blks) in side_weights:
        rows, cols = arr.shape[0], arr.shape[1] // n_col_blks
        n_chunks = rows // chunk_rows
        assert n_chunks * chunk_rows == rows and n_chunks <= n_tiles * n_i
        in_map = lambda n, i, last=n_chunks - 1, cb_=col_blk: (jnp.minimum(n * n_i + i, last), cb_)
        out_map = lambda n, i, last=n_chunks - 1: (jnp.minimum(n * n_i + i, last), 0)
        side_in_specs.append(pl.BlockSpec((chunk_rows, cols), in_map))
        side_specs.append(pl.BlockSpec((chunk_rows, cols), out_map))
        side_shapes.append(jax.ShapeDtypeStruct((rows, cols), BF16))
    if fuse_norm:
        lhs_specs = [pl.BlockSpec((None, tm, d), lambda n, i: (0, i, 0)),
                     pl.BlockSpec((1, d), lambda n, i: (0, 0))]
        lhs_args = (lhs, gain)
        extra_specs = [pl.BlockSpec((tm, d), lambda n, i: (i, 0))]
        extra_shapes = [jax.ShapeDtypeStruct((s, d), BF16)]
        w_mode = dict(pipeline_mode=pl.Buffered(1))
    else:
        lhs_specs = [pl.BlockSpec((tm, d), lambda n, i: (i, 0))]
        lhs_args = (lhs,)
        extra_specs, extra_shapes, w_mode = [], [], {}
    const = lambda n, i: (0, 0)
    outs = pl.pallas_call(
        kern,
        grid=(n_tiles, n_i),
        in_specs=[pl.BlockSpec(memory_space=pltpu.SMEM)] + lhs_specs + [
            pl.BlockSpec((None, d, tn), lambda n, i: (0, 0, n + tile0), **w_mode),
            pl.BlockSpec((tm, head_w), const),
            pl.BlockSpec((tm, head_w), const),
            pl.BlockSpec((None, 1, head_w), lambda n, i: (i, 0, 0)),
            pl.BlockSpec((None, 1, head_w), lambda n, i: (i, 0, 0))] + side_in_specs,
        out_specs=[pl.BlockSpec((tm, tn), lambda n, i: (i, n))] + extra_specs + side_specs,
        out_shape=[jax.ShapeDtypeStruct((s, n_tiles * tn), BF16)] + extra_shapes + side_shapes,
        scratch_shapes=[pltpu.VMEM((d, tn), BF16),
                        pltpu.VMEM((tm, head_w), F32),
                        pltpu.VMEM((tm, head_w), F32)],
        compiler_params=_cparams(2),
        name="ret_inproj_first" if fuse_norm else "ret_inproj",
    )(log_gamma, *lhs_args, w_in, cr, sr, ct, st, *[sw[0] for sw in side_weights])
    return outs[0], outs[1:]


def _ret_outproj_kernel(lg_ref, first_ref, rest_ref, x_ref, wout_ref, o_ref, state_ref, dmat_ref,
                        *, heads, dk, dv):
    t = pl.program_id(0)
    cb, n_first = first_ref.shape

    def cols(start, width):
        if start + width <= n_first:
            return first_ref[:, start:start + width]
        assert start >= n_first
        return rest_ref[:, start - n_first:start - n_first + width]

    @pl.when(t == 0)
    def _():
        state_ref[...] = jnp.zeros_like(state_ref)
        r = lax.broadcasted_iota(jnp.int32, (cb, cb), 0)
        c = lax.broadcasted_iota(jnp.int32, (cb, cb), 1)
        for h in range(heads):
            undo = jnp.exp(jnp.full((cb, cb), -cb, F32) * lg_ref[h])
            dmat_ref[h] = jnp.where(r >= c, undo, 0.0)

    k0, v0, g0 = heads * dk, 2 * heads * dk, 2 * heads * dk + heads * dv

    def mix(h):
        qd = cols(h * dk, dk)
        kd = cols(k0 + h * dk, dk)
        v = cols(v0 + h * dv, dv)
        block_decay = jnp.exp(jnp.full((1, 1), cb, F32) * lg_ref[h])
        scores = lax.dot_general(qd, kd, (((1,), (1,)), ((), ())), preferred_element_type=F32)
        state = state_ref[h]
        cross = jnp.dot(qd, state.astype(BF16), preferred_element_type=F32)
        update = lax.dot_general(kd, v, (((0,), (0,)), ((), ())), preferred_element_type=F32)
        p = (scores * dmat_ref[h]).astype(BF16)
        y = cross + jnp.dot(p, v, preferred_element_type=F32)
        state_ref[h] = state * block_decay + update
        return y

    def norm_gate(h, y):
        gate = cols(g0 + h * dv, dv)
        mu = jnp.mean(y, axis=-1, keepdims=True)
        yc = y - mu
        var = jnp.mean(yc * yc, axis=-1, keepdims=True)
        return (yc * lax.rsqrt(var + GN_EPS)).astype(BF16) * gate

    def project(h, yg, acc):
        return acc + jnp.dot(yg, wout_ref[h * dv:(h + 1) * dv, :], preferred_element_type=F32)

    acc = x_ref[...]
    ys, ygs = {}, {}
    for step in range(heads + 2):
        if 0 <= step - 2 < heads:
            acc = project(step - 2, ygs.pop(step - 2), acc)
        if step < heads:
            ys[step] = mix(step)
        if 0 <= step - 1 < heads:
            ygs[step - 1] = norm_gate(step - 1, ys.pop(step - 1))
    o_ref[...] = acc


def _ret_outproj(qkvg_first, qkvg_rest, x, w_out, log_gamma, heads, dk, dv, cb=RET_BLOCK):
    s, n_first = qkvg_first.shape
    n_rest = qkvg_rest.shape[1]
    d = w_out.shape[1]
    return pl.pallas_call(
        functools.partial(_ret_outproj_kernel, heads=heads, dk=dk, dv=dv),
        grid=(s // cb,),
        in_specs=[pl.BlockSpec(memory_space=pltpu.SMEM),
                  pl.BlockSpec((cb, n_first), lambda t: (t, 0)),
                  pl.BlockSpec((cb, n_rest), lambda t: (t, 0)),
                  pl.BlockSpec((None, cb, d), lambda t: (0, t, 0)),
                  pl.BlockSpec((heads * dv, d), lambda t: (0, 0), pipeline_mode=pl.Buffered(1))],
        out_specs=pl.BlockSpec((cb, d), lambda t: (t, 0)),
        out_shape=jax.ShapeDtypeStruct((s, d), F32),
        scratch_shapes=[pltpu.VMEM((heads, dk, dv), F32), pltpu.VMEM((heads, cb, cb), F32)],
        compiler_params=_cparams(1),
        name="ret_outproj",
    )(log_gamma, qkvg_first, qkvg_rest, x, w_out)


def _ple_kernel(h_ref, p_ref, pn_ref, wgate_ref, wproj_ref, nn_ref, h_out_ref, xn_out_ref, *,
                sub):
    n_sub = h_ref.shape[0] // sub
    rows = [slice(r * sub, (r + 1) * sub) for r in range(n_sub)]

    def normed(r):
        return _rms(h_ref[rows[r], :], pn_ref[...]).astype(BF16)

    def matmuls(r, xn):
        logits = jnp.dot(xn, wgate_ref[...], preferred_element_type=F32)
        emb = jnp.dot(p_ref[rows[r], :].astype(BF16), wproj_ref[...], preferred_element_type=F32)
        return logits, emb

    def finish(r, logits, emb):
        h2 = h_ref[rows[r], :] + _sigmoid(logits) * emb
        h_out_ref[rows[r], :] = h2
        xn_out_ref[rows[r], :] = _rms(h2, nn_ref[...]).astype(BF16)

    xns, mms = {}, {}
    for step in range(n_sub + 2):
        if step < n_sub:
            xns[step] = normed(step)
        if 0 <= step - 1 < n_sub:
            mms[step - 1] = matmuls(step - 1, xns.pop(step - 1))
        if 0 <= step - 2 < n_sub:
            finish(step - 2, *mms.pop(step - 2))


def _ple(h, p, layer, ple_norm, w_gate, w_proj, next_norm, tm=512, sub=256):
    s, d = h.shape
    pd = p.shape[-1]
    row = lambda i: (i, 0)
    resident = lambda shape, idx: pl.BlockSpec(shape, lambda i: idx, pipeline_mode=pl.Buffered(1))
    return pl.pallas_call(
        functools.partial(_ple_kernel, sub=sub),
        grid=(s // tm,),
        in_specs=[pl.BlockSpec((tm, d), row),
                  pl.BlockSpec((None, None, tm, pd), lambda i: (layer, 0, i, 0)),
                  resident((None, 1, d), (layer, 0, 0)),
                  resident((None, d, d), (layer, 0, 0)),
                  resident((None, pd, d), (layer, 0, 0)),
                  resident((1, d), (0, 0))],
        out_specs=(pl.BlockSpec((tm, d), row), pl.BlockSpec((tm, d), row)),
        out_shape=(jax.ShapeDtypeStruct((s, d), F32), jax.ShapeDtypeStruct((s, d), BF16)),
        compiler_params=_cparams(1),
        name="ple",
    )(h, p, ple_norm, w_gate, w_proj, next_norm)


def _outproj_ple_final_kernel(z_ref, h_ref, p_ref, wout_ref, pn_ref, wgate_ref, wproj_ref,
                              fn_ref, o_ref):
    h1 = h_ref[...] + jnp.dot(z_ref[...], wout_ref[...], preferred_element_type=F32)
    xn = _rms(h1, pn_ref[...]).astype(BF16)
    gate = _sigmoid(jnp.dot(xn, wgate_ref[...], preferred_element_type=F32))
    emb = jnp.dot(p_ref[...].astype(BF16), wproj_ref[...], preferred_element_type=F32)
    o_ref[...] = _rms(h1 + gate * emb, fn_ref[...])


def _outproj_ple_final(z, h, p, layer, w_out, ple_norm, w_gate, w_proj, final_norm, tm=256):
    s, d = h.shape
    kz = z.shape[1]
    pd = p.shape[-1]
    row = lambda i: (i, 0)
    resident = lambda shape, idx: pl.BlockSpec(shape, lambda i: idx, pipeline_mode=pl.Buffered(1))
    return pl.pallas_call(
        _outproj_ple_final_kernel,
        grid=(s // tm,),
        in_specs=[pl.BlockSpec((tm, kz), row),
                  pl.BlockSpec((tm, d), row),
                  pl.BlockSpec((None, None, tm, pd), lambda i: (layer, 0, i, 0)),
                  resident((kz, d), (0, 0)),
                  resident((None, 1, d), (layer, 0, 0)),
                  resident((None, d, d), (layer, 0, 0)),
                  resident((None, pd, d), (layer, 0, 0)),
                  resident((1, d), (0, 0))],
        out_specs=pl.BlockSpec((None, tm, d), lambda i: (0, i, 0)),
        out_shape=jax.ShapeDtypeStruct((1, s, d), F32),
        compiler_params=_cparams(1),
        name="outproj_ple_final",
    )(z, h, p, w_out, ple_norm, w_gate, w_proj, final_norm)


def _pool_wfuse_kernel(wu_ref, wgrp_ref, o_ref):
    o_ref[...] = jnp.dot(wu_ref[...].astype(BF16), wgrp_ref[...].astype(BF16),
                         preferred_element_type=F32).astype(o_ref.dtype)


def _pool_wfuse(w_in, w_group):
    _, d, _ = w_in.shape
    _, n_groups, gd, _ = w_group.shape
    return pl.pallas_call(
        _pool_wfuse_kernel,
        grid=(n_groups,),
        in_specs=[pl.BlockSpec((None, d, gd), lambda g: (0, 0, g)),
                  pl.BlockSpec((None, None, gd, gd), lambda g: (0, g, 0, 0))],
        out_specs=pl.BlockSpec((d, gd), lambda g: (0, g)),
        out_shape=jax.ShapeDtypeStruct((d, n_groups * gd), BF16),
        compiler_params=_cparams(1),
        name="pool_wfuse",
    )(w_in, w_group)


def _pool_kernel(xn_ref, wf_ref, wg_ref, scale_ref, o_ref, xbuf_ref, s2_ref, s4_ref, s8_ref, *,
                 windows):
    i = pl.program_id(0)
    tm = xn_ref.shape[0]
    gd = o_ref.shape[1] // len(windows)
    hl = POOL_HALO
    end = hl + tm

    @pl.when(i == 0)
    def _():
        xbuf_ref[0:hl, :] = jnp.zeros((hl, xbuf_ref.shape[1]), F32)

    xn = xn_ref[...]
    x32 = xn.astype(F32)
    xbuf_ref[hl:end, :] = x32
    s2_ref[8:end, :] = xbuf_ref[8:end, :] + xbuf_ref[7:end - 1, :]
    s4_ref[16:end, :] = s2_ref[16:end, :] + s2_ref[14:end - 2, :]
    s8_ref[24:end, :] = s4_ref[24:end, :] + s4_ref[20:end - 4, :]
    wsums = {2: s2_ref[hl:end, :], 4: s4_ref[hl:end, :], 8: s8_ref[hl:end, :],
             16: s8_ref[hl:end, :] + s8_ref[hl - 8:end - 8, :]}
    xbuf_ref[0:hl, :] = xbuf_ref[tm:end, :]

    tok = i * tm + lax.broadcasted_iota(jnp.int32, (tm, 1), 0)
    for g, w in enumerate(windows):
        cols = slice(g * gd, (g + 1) * gd)
        gate = _silu(jnp.dot(xn, wg_ref[:, cols], preferred_element_type=F32))
        inv_cnt = 1.0 / jnp.minimum(tok + 1, w).astype(F32)
        dev = (wsums[w] * inv_cnt - x32).astype(BF16)
        mixed = jnp.dot(dev, wf_ref[:, cols], preferred_element_type=F32) * scale_ref[:, cols]
        o_ref[:, cols] = (mixed * gate).astype(o_ref.dtype)


def _pool_mixer(xn, w_fused, w_gate, scale, tm=256):
    s, d = xn.shape
    width = w_fused.shape[1]
    assert set(POOL_WINDOWS) == {2, 4, 8, 16} and POOL_HALO >= 2 * max(POOL_WINDOWS)
    halo_buf = pltpu.VMEM((POOL_HALO + tm, d), F32)
    resident = lambda shape: pl.BlockSpec(shape, lambda i: (0, 0), pipeline_mode=pl.Buffered(1))
    return pl.pallas_call(
        functools.partial(_pool_kernel, windows=POOL_WINDOWS),
        grid=(s // tm,),
        in_specs=[pl.BlockSpec((tm, d), lambda i: (i, 0)),
                  resident((d, width)),
                  resident((d, width)),
                  resident((1, width))],
        out_specs=pl.BlockSpec((tm, width), lambda i: (i, 0)),
        out_shape=jax.ShapeDtypeStruct((s, width), BF16),
        scratch_shapes=[halo_buf, halo_buf, halo_buf, halo_buf],
        compiler_params=_cparams(1),
        name="pool_mixer",
    )(xn, w_fused, w_gate, scale)


def _rotary_tables(seq, dk, tm):
    freq = ROPE_BASE ** (-jnp.linspace(0.0, 1.0, dk // 2, dtype=F32))
    ang_r = jnp.arange(tm, dtype=F32)[:, None] * freq[None, :]
    ang_t = (jnp.arange(seq // tm, dtype=F32) * tm)[:, None, None] * freq[None, None, :]
    return jnp.cos(ang_r), jnp.sin(ang_r), jnp.cos(ang_t), jnp.sin(ang_t)


def kernel(x, p, ret_norm, ret_w_in, ret_w_out, pool_norm, pool_w_in, pool_w_group, pool_scale,
           pool_w_out, ple_norm, ple_w_gate, ple_w_proj, final_norm):
    b, s, d = x.shape
    assert b == 1
    heads = RET_HEADS
    dk = d // heads
    dv = ret_w_out.shape[1] // heads
    assert ret_w_in.shape[2] == 2 * heads * dk + 2 * heads * dv
    tm_inproj = tn_inproj = 1024

    tabs = _rotary_tables(s, dk, tm_inproj)
    log_gamma = jnp.log1p(-(2.0 ** (-5.0 - jnp.arange(heads, dtype=F32))))
    ple_norm3 = ple_norm.reshape(ple_norm.shape[0], 1, d)
    n_layers, pd, _ = ple_w_proj.shape
    n_groups, gd = pool_w_group.shape[1:3]
    pool_width = n_groups * gd

    whole = (0, 1)
    side = [(ret_w_out.reshape(heads * dv, d), 64, whole),
            (pool_w_in.reshape(d, 2 * pool_width), 32, (1, 2)),
            (pool_w_out.reshape(pool_width, d), 64, whole),
            (ple_w_gate.reshape(n_layers * d, d), 64, whole),
            (ple_w_proj.reshape(n_layers * pd, d), 32, whole)]

    n_tiles = ret_w_in.shape[2] // tn_inproj
    widths = (heads, heads * dk, heads * dv, dk ** -0.5, tm_inproj)
    qkvg_first, (xn0,) = _ret_inproj(x, ret_w_in, tabs, log_gamma, [], *widths, 0, 1, gain=ret_norm)
    qkvg_rest, (w_ret_out, w_pool_gate, w_pool_out, w_gate, w_proj) = _ret_inproj(
        xn0, ret_w_in, tabs, log_gamma, side, *widths, 1, n_tiles - 1)
    w_gate = w_gate.reshape(n_layers, d, d)
    w_proj = w_proj.reshape(n_layers, pd, d)
    h1 = _ret_outproj(qkvg_first, qkvg_rest, x, w_ret_out, log_gamma, heads, dk, dv)
    h2, xn2 = _ple(h1, p, 0, ple_norm3, w_gate, w_proj, pool_norm)

    w_pool_fused = _pool_wfuse(pool_w_in, pool_w_group)
    z = _pool_mixer(xn2, w_pool_fused, w_pool_gate, pool_scale)
    return _outproj_ple_final(z, h2, p, 1, w_pool_out, ple_norm3, w_gate, w_proj,
                              final_norm.reshape(1, d))
```

```python
import functools

import jax
import jax.numpy as jnp
from jax import lax
from jax.experimental import pallas as pl
from jax.experimental.pallas import tpu as pltpu

F32 = jnp.float32
BF16 = jnp.bfloat16

RET_HEADS = 8
ROPE_BASE = 10000.0
POOL_WINDOWS = (2, 4, 8, 16)
NORM_EPS = 1e-6
GN_EPS = 1e-5

LANES = 128
SUBLANES = 8
MXU_WIDTH = 256
POOL_HALO = 32
RET_BLOCK = 256
VMEM_LIMIT = 56 * 1024 * 1024


def _cparams(n_axes):
    return pltpu.CompilerParams(
        dimension_semantics=("arbitrary",) * n_axes,
        vmem_limit_bytes=VMEM_LIMIT,
    )


def _rms(x, gain):
    ms = jnp.mean(x * x, axis=-1, keepdims=True)
    return x * lax.rsqrt(ms + NORM_EPS) * gain


def _sigmoid(x):
    return 0.5 * jnp.tanh(0.5 * x) + 0.5


def _silu(x):
    h = 0.5 * x
    return h * jnp.tanh(h) + h


def _ret_inproj_kernel(lg_ref, lhs_ref, *rest, fuse_norm, n_side, tile0, n_tiles, heads,
                       n_q_tiles, n_rot_tiles, n_plain_tiles, k_scale, cb):
    if fuse_norm:
        gain_ref, rest = rest[0], rest[1:]
    w_ref, cr_ref, sr_ref, ct_ref, st_ref = rest[:5]
    rest = rest[5:]
    side_in, rest = rest[:n_side], rest[n_side:]
    o_ref, rest = rest[0], rest[1:]
    if fuse_norm:
        xn_ref, rest = rest[0], rest[1:]
    else:
        xn_ref = lhs_ref
    side_out, rest = rest[:n_side], rest[n_side:]
    wbf_ref, cs_ref, sn_ref = rest
    n = tile0 if n_tiles == 1 else pl.program_id(0) + tile0
    i = pl.program_id(1)
    tm, tn = o_ref.shape
    heads_per_tile = tn // MXU_WIDTH

    def when(cond):
        if isinstance(cond, bool):
            return lambda f: f() if cond else None
        return pl.when(cond)

    @when(n >= n_rot_tiles)
    def _():
        @pl.when(i == 0)
        def _():
            wbf_ref[...] = w_ref[...].astype(BF16)

    @when(n < n_rot_tiles)
    def _():
        @pl.when(i == 0)
        def _():
            src = lax.broadcasted_iota(jnp.int32, (MXU_WIDTH, MXU_WIDTH), 0)
            dst = lax.broadcasted_iota(jnp.int32, (MXU_WIDTH, MXU_WIDTH), 1)
            half = MXU_WIDTH // 2
            wanted = jnp.where(dst < half, 2 * dst, 2 * (dst - half) + 1)
            perm = (src == wanted).astype(BF16)
            for c in range(heads_per_tile):
                cols = slice(c * MXU_WIDTH, (c + 1) * MXU_WIDTH)
                wbf_ref[:, cols] = jnp.dot(w_ref[:, cols].astype(BF16), perm,
                                           preferred_element_type=F32).astype(BF16)

    if fuse_norm:
        xn_ref[...] = _rms(lhs_ref[...], gain_ref[...]).astype(xn_ref.dtype)

    def run(epilogue):
        for src, dst in zip(side_in, side_out):
            dst[...] = src[...].astype(BF16)
        for c in range(heads_per_tile):
            cols = slice(c * MXU_WIDTH, (c + 1) * MXU_WIDTH)
            acc = jnp.dot(xn_ref[...], wbf_ref[:, cols], preferred_element_type=F32)
            o_ref[:, cols] = epilogue(acc, c).astype(o_ref.dtype)

    @when(n < n_rot_tiles)
    def _():
        ct, st = ct_ref[...], st_ref[...]
        cr, sr = cr_ref[...], sr_ref[...]
        cs_ref[...] = ct * cr - st * sr
        sn_ref[...] = st * cr + ct * sr
        is_k = n >= n_q_tiles
        row = lax.broadcasted_iota(jnp.int32, (tm, LANES), 0)
        j = (row & (cb - 1)).astype(F32)
        expo = jnp.where(is_k, cb - 1.0 - j, j + 1.0)
        scale = jnp.where(is_k, k_scale, 1.0).astype(F32)

        def rotate(acc, c):
            head = (n * heads_per_tile + c) % heads
            decay = jnp.exp(expo * lg_ref[head]) * scale
            cd = cs_ref[...] * decay
            sd = sn_ref[...] * decay
            xe, xo = acc[:, :LANES], acc[:, LANES:]
            return jnp.concatenate([xe * cd - xo * sd, xo * cd + xe * sd], axis=1)

        run(rotate)

    @when((n >= n_rot_tiles) & (n < n_plain_tiles))
    def _():
        run(lambda acc, c: acc)

    @when(n >= n_plain_tiles)
    def _():
        run(lambda acc, c: _silu(acc))


def _ret_inproj(lhs, w_in, tabs, log_gamma, side_weights, heads, qk_width, v_width, k_scale, tm,
                tile0, n_tiles, gain=None, tn=1024, cb=RET_BLOCK):
    fuse_norm = gain is not None
    assert not (fuse_norm and (side_weights or n_tiles != 1))
    s, d = lhs.shape[-2:]
    cr, sr, ct, st = tabs
    head_w = cr.shape[1]
    assert 2 * head_w == MXU_WIDTH == 2 * LANES and cr.shape[0] == tm
    assert tm % cb == 0 and cb & (cb - 1) == 0
    n_i = s // tm
    n_rot_tiles = 2 * qk_width // tn
    kern = functools.partial(_ret_inproj_kernel, fuse_norm=fuse_norm, n_side=len(side_weights),
                             tile0=tile0, n_tiles=n_tiles, heads=heads, n_q_tiles=qk_width // tn,
                             n_rot_tiles=n_rot_tiles, n_plain_tiles=n_rot_tiles + v_width // tn,
                             k_scale=k_scale, cb=cb)
    side_in_specs, side_specs, side_shapes = [], [], []
    for arr, chunk_rows, (col_blk, n_col_blks) in side_weights:
        rows, cols = arr.shape[0], arr.shape[1] // n_col_blks
        n_chunks = rows // chunk_rows
        assert n_chunks * chunk_rows == rows and n_chunks <= n_tiles * n_i
        in_map = lambda n, i, last=n_chunks - 1, cb_=col_blk: (jnp.minimum(n * n_i + i, last), cb_)
        out_map = lambda n, i, last=n_chunks - 1: (jnp.minimum(n * n_i + i, last), 0)
        side_in_specs.append(pl.BlockSpec((chunk_rows, cols), in_map))
        side_specs.append(pl.BlockSpec((chunk_rows, cols), out_map))
        side_shapes.append(jax.ShapeDtypeStruct((rows, cols), BF16))
    if fuse_norm:
        lhs_specs = [pl.BlockSpec((None, tm, d), lambda n, i: (0, i, 0)),
                     pl.BlockSpec((1, d), lambda n, i: (0, 0))]
        lhs_args = (lhs, gain)
        extra_specs = [pl.BlockSpec((tm, d), lambda n, i: (i, 0))]
        extra_shapes = [jax.ShapeDtypeStruct((s, d), BF16)]
        w_mode = dict(pipeline_mode=pl.Buffered(1))
    else:
        lhs_specs = [pl.BlockSpec((tm, d), lambda n, i: (i, 0))]
        lhs_args = (lhs,)
        extra_specs, extra_shapes, w_mode = [], [], {}
    const = lambda n, i: (0, 0)
    outs = pl.pallas_call(
        kern,
        grid=(n_tiles, n_i),
        in_specs=[pl.BlockSpec(memory_space=pltpu.SMEM)] + lhs_specs + [
            pl.BlockSpec((None, d, tn), lambda n, i: (0, 0, n + tile0), **w_mode),
            pl.BlockSpec((tm, head_w), const),
            pl.BlockSpec((tm, head_w), const),
            pl.BlockSpec((None, 1, head_w), lambda n, i: (i, 0, 0)),
            pl.BlockSpec((None, 1, head_w), lambda n, i: (i, 0, 0))] + side_in_specs,
        out_specs=[pl.BlockSpec((tm, tn), lambda n, i: (i, n))] + extra_specs + side_specs,
        out_shape=[jax.ShapeDtypeStruct((s, n_tiles * tn), BF16)] + extra_shapes + side_shapes,
        scratch_shapes=[pltpu.VMEM((d, tn), BF16),
                        pltpu.VMEM((tm, head_w), F32),
                        pltpu.VMEM((tm, head_w), F32)],
        compiler_params=_cparams(2),
        name="ret_inproj_first" if fuse_norm else "ret_inproj",
    )(log_gamma, *lhs_args, w_in, cr, sr, ct, st, *[sw[0] for sw in side_weights])
    return outs[0], outs[1:]


def _ret_outproj_kernel(lg_ref, first_ref, rest_ref, x_ref, wout_ref, o_ref, state_ref, dmat_ref,
                        *, heads, dk, dv):
    t = pl.program_id(0)
    cb, n_first = first_ref.shape

    def cols(start, width):
        if start + width <= n_first:
            return first_ref[:, start:start + width]
        assert start >= n_first
        return rest_ref[:, start - n_first:start - n_first + width]

    @pl.when(t == 0)
    def _():
        state_ref[...] = jnp.zeros_like(state_ref)
        r = lax.broadcasted_iota(jnp.int32, (cb, cb), 0)
        c = lax.broadcasted_iota(jnp.int32, (cb, cb), 1)
        for h in range(heads):
            undo = jnp.exp(jnp.full((cb, cb), -cb, F32) * lg_ref[h])
            dmat_ref[h] = jnp.where(r >= c, undo, 0.0)

    k0, v0, g0 = heads * dk, 2 * heads * dk, 2 * heads * dk + heads * dv

    def mix(h):
        qd = cols(h * dk, dk)
        kd = cols(k0 + h * dk, dk)
        v = cols(v0 + h * dv, dv)
        block_decay = jnp.exp(jnp.full((1, 1), cb, F32) * lg_ref[h])
        scores = lax.dot_general(qd, kd, (((1,), (1,)), ((), ())), preferred_element_type=F32)
        state = state_ref[h]
        cross = jnp.dot(qd, state.astype(BF16), preferred_element_type=F32)
        update = lax.dot_general(kd, v, (((0,), (0,)), ((), ())), preferred_element_type=F32)
        p = (scores * dmat_ref[h]).astype(BF16)
        y = cross + jnp.dot(p, v, preferred_element_type=F32)
        state_ref[h] = state * block_decay + update
        return y

    def norm_gate(h, y):
        gate = cols(g0 + h * dv, dv)
        mu = jnp.mean(y, axis=-1, keepdims=True)
        yc = y - mu
        var = jnp.mean(yc * yc, axis=-1, keepdims=True)
        return (yc * lax.rsqrt(var + GN_EPS)).astype(BF16) * gate

    def project(h, yg, acc):
        return acc + jnp.dot(yg, wout_ref[h * dv:(h + 1) * dv, :], preferred_element_type=F32)

    acc = x_ref[...]
    ys, ygs = {}, {}
    for step in range(heads + 2):
        if 0 <= step - 2 < heads:
            acc = project(step - 2, ygs.pop(step - 2), acc)
        if step < heads:
            ys[step] = mix(step)
        if 0 <= step - 1 < heads:
            ygs[step - 1] = norm_gate(step - 1, ys.pop(step - 1))
    o_ref[...] = acc


def _ret_outproj(qkvg_first, qkvg_rest, x, w_out, log_gamma, heads, dk, dv, cb=RET_BLOCK):
    s, n_first = qkvg_first.shape
    n_rest = qkvg_rest.shape[1]
    d = w_out.shape[1]
    return pl.pallas_call(
        functools.partial(_ret_outproj_kernel, heads=heads, dk=dk, dv=dv),
        grid=(s // cb,),
        in_specs=[pl.BlockSpec(memory_space=pltpu.SMEM),
                  pl.BlockSpec((cb, n_first), lambda t: (t, 0)),
                  pl.BlockSpec((cb, n_rest), lambda t: (t, 0)),
                  pl.BlockSpec((None, cb, d), lambda t: (0, t, 0)),
                  pl.BlockSpec((heads * dv, d), lambda t: (0, 0), pipeline_mode=pl.Buffered(1))],
        out_specs=pl.BlockSpec((cb, d), lambda t: (t, 0)),
        out_shape=jax.ShapeDtypeStruct((s, d), F32),
        scratch_shapes=[pltpu.VMEM((heads, dk, dv), F32), pltpu.VMEM((heads, cb, cb), F32)],
        compiler_params=_cparams(1),
        name="ret_outproj",
    )(log_gamma, qkvg_first, qkvg_rest, x, w_out)


def _ple_kernel(h_ref, p_ref, pn_ref, wgate_ref, wproj_ref, nn_ref, h_out_ref, xn_out_ref, *,
                sub):
    for r in range(h_ref.shape[0] // sub):
        rows = slice(r * sub, (r + 1) * sub)
        h1 = h_ref[rows, :]
        xn = _rms(h1, pn_ref[...]).astype(BF16)
        gate = _sigmoid(jnp.dot(xn, wgate_ref[...], preferred_element_type=F32))
        emb = jnp.dot(p_ref[rows, :].astype(BF16), wproj_ref[...], preferred_element_type=F32)
        h2 = h1 + gate * emb
        h_out_ref[rows, :] = h2
        xn_out_ref[rows, :] = _rms(h2, nn_ref[...]).astype(BF16)


def _ple(h, p, layer, ple_norm, w_gate, w_proj, next_norm, tm=512, sub=256):
    s, d = h.shape
    pd = p.shape[-1]
    row = lambda i: (i, 0)
    resident = lambda shape, idx: pl.BlockSpec(shape, lambda i: idx, pipeline_mode=pl.Buffered(1))
    return pl.pallas_call(
        functools.partial(_ple_kernel, sub=sub),
        grid=(s // tm,),
        in_specs=[pl.BlockSpec((tm, d), row),
                  pl.BlockSpec((None, None, tm, pd), lambda i: (layer, 0, i, 0)),
                  resident((None, 1, d), (layer, 0, 0)),
                  resident((None, d, d), (layer, 0, 0)),
                  resident((None, pd, d), (layer, 0, 0)),
                  resident((1, d), (0, 0))],
        out_specs=(pl.BlockSpec((tm, d), row), pl.BlockSpec((tm, d), row)),
        out_shape=(jax.ShapeDtypeStruct((s, d), F32), jax.ShapeDtypeStruct((s, d), BF16)),
        compiler_params=_cparams(1),
        name="ple",
    )(h, p, ple_norm, w_gate, w_proj, next_norm)


def _outproj_ple_final_kernel(z_ref, h_ref, p_ref, wout_ref, pn_ref, wgate_ref, wproj_ref,
                              fn_ref, o_ref):
    h1 = h_ref[...] + jnp.dot(z_ref[...], wout_ref[...], preferred_element_type=F32)
    xn = _rms(h1, pn_ref[...]).astype(BF16)
    gate = _sigmoid(jnp.dot(xn, wgate_ref[...], preferred_element_type=F32))
    emb = jnp.dot(p_ref[...].astype(BF16), wproj_ref[...], preferred_element_type=F32)
    o_ref[...] = _rms(h1 + gate * emb, fn_ref[...])


def _outproj_ple_final(z, h, p, layer, w_out, ple_norm, w_gate, w_proj, final_norm, tm=256):
    s, d = h.shape
    kz = z.shape[1]
    pd = p.shape[-1]
    row = lambda i: (i, 0)
    resident = lambda shape, idx: pl.BlockSpec(shape, lambda i: idx, pipeline_mode=pl.Buffered(1))
    return pl.pallas_call(
        _outproj_ple_final_kernel,
        grid=(s // tm,),
        in_specs=[pl.BlockSpec((tm, kz), row),
                  pl.BlockSpec((tm, d), row),
                  pl.BlockSpec((None, None, tm, pd), lambda i: (layer, 0, i, 0)),
                  resident((kz, d), (0, 0)),
                  resident((None, 1, d), (layer, 0, 0)),
                  resident((None, d, d), (layer, 0, 0)),
                  resident((None, pd, d), (layer, 0, 0)),
                  resident((1, d), (0, 0))],
        out_specs=pl.BlockSpec((None, tm, d), lambda i: (0, i, 0)),
        out_shape=jax.ShapeDtypeStruct((1, s, d), F32),
        compiler_params=_cparams(1),
        name="outproj_ple_final",
    )(z, h, p, w_out, ple_norm, w_gate, w_proj, final_norm)


def _pool_wfuse_kernel(wu_ref, wgrp_ref, o_ref):
    o_ref[...] = jnp.dot(wu_ref[...].astype(BF16), wgrp_ref[...].astype(BF16),
                         preferred_element_type=F32).astype(o_ref.dtype)


def _pool_wfuse(w_in, w_group):
    _, d, _ = w_in.shape
    _, n_groups, gd, _ = w_group.shape
    return pl.pallas_call(
        _pool_wfuse_kernel,
        grid=(n_groups,),
        in_specs=[pl.BlockSpec((None, d, gd), lambda g: (0, 0, g)),
                  pl.BlockSpec((None, None, gd, gd), lambda g: (0, g, 0, 0))],
        out_specs=pl.BlockSpec((d, gd), lambda g: (0, g)),
        out_shape=jax.ShapeDtypeStruct((d, n_groups * gd), BF16),
        compiler_params=_cparams(1),
        name="pool_wfuse",
    )(w_in, w_group)


def _pool_kernel(xn_ref, wf_ref, wg_ref, scale_ref, o_ref, xbuf_ref, s2_ref, s4_ref, s8_ref, *,
                 windows):
    i = pl.program_id(0)
    tm = xn_ref.shape[0]
    gd = o_ref.shape[1] // len(windows)
    hl = POOL_HALO
    end = hl + tm

    @pl.when(i == 0)
    def _():
        xbuf_ref[0:hl, :] = jnp.zeros((hl, xbuf_ref.shape[1]), F32)

    xn = xn_ref[...]
    x32 = xn.astype(F32)
    xbuf_ref[hl:end, :] = x32
    s2_ref[8:end, :] = xbuf_ref[8:end, :] + xbuf_ref[7:end - 1, :]
    s4_ref[16:end, :] = s2_ref[16:end, :] + s2_ref[14:end - 2, :]
    s8_ref[24:end, :] = s4_ref[24:end, :] + s4_ref[20:end - 4, :]
    wsums = {2: s2_ref[hl:end, :], 4: s4_ref[hl:end, :], 8: s8_ref[hl:end, :],
             16: s8_ref[hl:end, :] + s8_ref[hl - 8:end - 8, :]}
    xbuf_ref[0:hl, :] = xbuf_ref[tm:end, :]

    tok = i * tm + lax.broadcasted_iota(jnp.int32, (tm, 1), 0)
    for g, w in enumerate(windows):
        cols = slice(g * gd, (g + 1) * gd)
        gate = jnp.dot(xn, wg_ref[:, cols], preferred_element_type=F32)
        gate = gate * _sigmoid(gate)
        inv_cnt = 1.0 / jnp.minimum(tok + 1, w).astype(F32)
        dev = (wsums[w] * inv_cnt - x32).astype(BF16)
        mixed = jnp.dot(dev, wf_ref[:, cols], preferred_element_type=F32) * scale_ref[:, cols]
        o_ref[:, cols] = (mixed * gate).astype(o_ref.dtype)


def _pool_mixer(xn, w_fused, w_gate, scale, tm=256):
    s, d = xn.shape
    width = w_fused.shape[1]
    assert set(POOL_WINDOWS) == {2, 4, 8, 16} and POOL_HALO >= 2 * max(POOL_WINDOWS)
    halo_buf = pltpu.VMEM((POOL_HALO + tm, d), F32)
    resident = lambda shape: pl.BlockSpec(shape, lambda i: (0, 0), pipeline_mode=pl.Buffered(1))
    return pl.pallas_call(
        functools.partial(_pool_kernel, windows=POOL_WINDOWS),
        grid=(s // tm,),
        in_specs=[pl.BlockSpec((tm, d), lambda i: (i, 0)),
                  resident((d, width)),
                  resident((d, width)),
                  resident((1, width))],
        out_specs=pl.BlockSpec((tm, width), lambda i: (i, 0)),
        out_shape=jax.ShapeDtypeStruct((s, width), BF16),
        scratch_shapes=[halo_buf, halo_buf, halo_buf, halo_buf],
        compiler_params=_cparams(1),
        name="pool_mixer",
    )(xn, w_fused, w_gate, scale)


def _rotary_tables(seq, dk, tm):
    freq = ROPE_BASE ** (-jnp.linspace(0.0, 1.0, dk // 2, dtype=F32))
    ang_r = jnp.arange(tm, dtype=F32)[:, None] * freq[None, :]
    ang_t = (jnp.arange(seq // tm, dtype=F32) * tm)[:, None, None] * freq[None, None, :]
    return jnp.cos(ang_r), jnp.sin(ang_r), jnp.cos(ang_t), jnp.sin(ang_t)


def kernel(x, p, ret_norm, ret_w_in, ret_w_out, pool_norm, pool_w_in, pool_w_group, pool_scale,
           pool_w_out, ple_norm, ple_w_gate, ple_w_proj, final_norm):
    b, s, d = x.shape
    assert b == 1
    heads = RET_HEADS
    dk = d // heads
    dv = ret_w_out.shape[1] // heads
    assert ret_w_in.shape[2] == 2 * heads * dk + 2 * heads * dv
    tm_inproj = tn_inproj = 1024

    tabs = _rotary_tables(s, dk, tm_inproj)
    log_gamma = jnp.log1p(-(2.0 ** (-5.0 - jnp.arange(heads, dtype=F32))))
    ple_norm3 = ple_norm.reshape(ple_norm.shape[0], 1, d)
    n_layers, pd, _ = ple_w_proj.shape
    n_groups, gd = pool_w_group.shape[1:3]
    pool_width = n_groups * gd

    whole = (0, 1)
    side = [(ret_w_out.reshape(heads * dv, d), 64, whole),
            (pool_w_in.reshape(d, 2 * pool_width), 32, (1, 2)),
            (pool_w_out.reshape(pool_width, d), 64, whole),
            (ple_w_gate.reshape(n_layers * d, d), 64, whole),
            (ple_w_proj.reshape(n_layers * pd, d), 32, whole)]

    n_tiles = ret_w_in.shape[2] // tn_inproj
    widths = (heads, heads * dk, heads * dv, dk ** -0.5, tm_inproj)
    qkvg_first, (xn0,) = _ret_inproj(x, ret_w_in, tabs, log_gamma, [], *widths, 0, 1, gain=ret_norm)
    qkvg_rest, (w_ret_out, w_pool_gate, w_pool_out, w_gate, w_proj) = _ret_inproj(
        xn0, ret_w_in, tabs, log_gamma, side, *widths, 1, n_tiles - 1)
    w_gate = w_gate.reshape(n_layers, d, d)
    w_proj = w_proj.reshape(n_layers, pd, d)
    h1 = _ret_outproj(qkvg_first, qkvg_rest, x, w_ret_out, log_gamma, heads, dk, dv)
    h2, xn2 = _ple(h1, p, 0, ple_norm3, w_gate, w_proj, pool_norm)

    w_pool_fused = _pool_wfuse(pool_w_in, pool_w_group)
    z = _pool_mixer(xn2, w_pool_fused, w_pool_gate, pool_scale)
    return _outproj_ple_final(z, h2, p, 1, w_pool_out, ple_norm3, w_gate, w_proj,
                              final_norm.reshape(1, d))
```

```python
import functools

import jax
import jax.numpy as jnp
from jax import lax
from jax.experimental import pallas as pl
from jax.experimental.pallas import tpu as pltpu

F32 = jnp.float32
BF16 = jnp.bfloat16

RET_HEADS = 8
ROPE_BASE = 10000.0
POOL_WINDOWS = (2, 4, 8, 16)
NORM_EPS = 1e-6
GN_EPS = 1e-5

LANES = 128
SUBLANES = 8
MXU_WIDTH = 256
POOL_HALO = 32
RET_BLOCK = 256
VMEM_LIMIT = 56 * 1024 * 1024


def _cparams(n_axes):
    return pltpu.CompilerParams(
        dimension_semantics=("arbitrary",) * n_axes,
        vmem_limit_bytes=VMEM_LIMIT,
    )


def _rms(x, gain):
    ms = jnp.mean(x * x, axis=-1, keepdims=True)
    return x * lax.rsqrt(ms + NORM_EPS) * gain


def _sigmoid(x):
    return 0.5 * jnp.tanh(0.5 * x) + 0.5


def _silu(x):
    h = 0.5 * x
    return h * jnp.tanh(h) + h


def _ret_inproj_kernel(lg_ref, lhs_ref, *rest, fuse_norm, n_wchunks, n_side, tile0, n_tiles,
                       heads, n_q_tiles, n_rot_tiles, n_plain_tiles, k_scale, cb):
    if fuse_norm:
        gain_ref, rest = rest[0], rest[1:]
    w_refs, rest = rest[:n_wchunks], rest[n_wchunks:]
    cr_ref, sr_ref, ct_ref, st_ref = rest[:4]
    rest = rest[4:]
    side_in, rest = rest[:n_side], rest[n_side:]
    o_ref, rest = rest[0], rest[1:]
    if fuse_norm:
        xn_ref, rest = rest[0], rest[1:]
    else:
        xn_ref = lhs_ref
    side_out, rest = rest[:n_side], rest[n_side:]
    wbf_ref, cs_ref, sn_ref = rest
    n = tile0 if n_tiles == 1 else pl.program_id(0) + tile0
    i = pl.program_id(1)
    tm, tn = o_ref.shape
    heads_per_tile = tn // MXU_WIDTH
    wrows = wbf_ref.shape[0] // n_wchunks

    def when(cond):
        if isinstance(cond, bool):
            return lambda f: f() if cond else None
        return pl.when(cond)

    @when(n >= n_rot_tiles)
    def _():
        @pl.when(i == 0)
        def _():
            for k, w_ref in enumerate(w_refs):
                wbf_ref[k * wrows:(k + 1) * wrows, :] = w_ref[...].astype(BF16)

    @when(n < n_rot_tiles)
    def _():
        @pl.when(i == 0)
        def _():
            src = lax.broadcasted_iota(jnp.int32, (MXU_WIDTH, MXU_WIDTH), 0)
            dst = lax.broadcasted_iota(jnp.int32, (MXU_WIDTH, MXU_WIDTH), 1)
            half = MXU_WIDTH // 2
            wanted = jnp.where(dst < half, 2 * dst, 2 * (dst - half) + 1)
            perm = (src == wanted).astype(BF16)
            for k, w_ref in enumerate(w_refs):
                for c in range(heads_per_tile):
                    cols = slice(c * MXU_WIDTH, (c + 1) * MXU_WIDTH)
                    wbf_ref[k * wrows:(k + 1) * wrows, cols] = jnp.dot(
                        w_ref[:, cols].astype(BF16), perm,
                        preferred_element_type=F32).astype(BF16)

    if fuse_norm:
        xn_ref[...] = _rms(lhs_ref[...], gain_ref[...]).astype(xn_ref.dtype)

    def run(epilogue):
        for src, dst in zip(side_in, side_out):
            dst[...] = src[...].astype(BF16)
        for c in range(heads_per_tile):
            cols = slice(c * MXU_WIDTH, (c + 1) * MXU_WIDTH)
            acc = jnp.dot(xn_ref[...], wbf_ref[:, cols], preferred_element_type=F32)
            o_ref[:, cols] = epilogue(acc, c).astype(o_ref.dtype)

    @when(n < n_rot_tiles)
    def _():
        ct, st = ct_ref[...], st_ref[...]
        cr, sr = cr_ref[...], sr_ref[...]
        cs_ref[...] = ct * cr - st * sr
        sn_ref[...] = st * cr + ct * sr
        is_k = n >= n_q_tiles
        row = lax.broadcasted_iota(jnp.int32, (tm, LANES), 0)
        j = (row & (cb - 1)).astype(F32)
        expo = jnp.where(is_k, cb - 1.0 - j, j + 1.0)
        scale = jnp.where(is_k, k_scale, 1.0).astype(F32)

        def rotate(acc, c):
            head = (n * heads_per_tile + c) % heads
            decay = jnp.exp(expo * lg_ref[head]) * scale
            cd = cs_ref[...] * decay
            sd = sn_ref[...] * decay
            xe, xo = acc[:, :LANES], acc[:, LANES:]
            return jnp.concatenate([xe * cd - xo * sd, xo * cd + xe * sd], axis=1)

        run(rotate)

    @when((n >= n_rot_tiles) & (n < n_plain_tiles))
    def _():
        run(lambda acc, c: acc)

    @when(n >= n_plain_tiles)
    def _():
        run(lambda acc, c: _silu(acc))


def _ret_inproj(lhs, w_in, tabs, log_gamma, side_weights, heads, qk_width, v_width, k_scale, tm,
                tile0, n_tiles, gain=None, tn=1024, cb=RET_BLOCK, n_wchunks=4):
    fuse_norm = gain is not None
    assert not (fuse_norm and (side_weights or n_tiles != 1))
    s, d = lhs.shape[-2:]
    cr, sr, ct, st = tabs
    head_w = cr.shape[1]
    assert 2 * head_w == MXU_WIDTH == 2 * LANES and cr.shape[0] == tm
    assert tm % cb == 0 and cb & (cb - 1) == 0
    n_i = s // tm
    n_rot_tiles = 2 * qk_width // tn
    kern = functools.partial(_ret_inproj_kernel, fuse_norm=fuse_norm, n_wchunks=n_wchunks,
                             n_side=len(side_weights),
                             tile0=tile0, n_tiles=n_tiles, heads=heads, n_q_tiles=qk_width // tn,
                             n_rot_tiles=n_rot_tiles, n_plain_tiles=n_rot_tiles + v_width // tn,
                             k_scale=k_scale, cb=cb)
    side_in_specs, side_specs, side_shapes = [], [], []
    for arr, chunk_rows, (col_blk, n_col_blks) in side_weights:
        rows, cols = arr.shape[0], arr.shape[1] // n_col_blks
        n_chunks = rows // chunk_rows
        assert n_chunks * chunk_rows == rows and n_chunks <= n_tiles * n_i
        in_map = lambda n, i, last=n_chunks - 1, cb_=col_blk: (jnp.minimum(n * n_i + i, last), cb_)
        out_map = lambda n, i, last=n_chunks - 1: (jnp.minimum(n * n_i + i, last), 0)
        side_in_specs.append(pl.BlockSpec((chunk_rows, cols), in_map))
        side_specs.append(pl.BlockSpec((chunk_rows, cols), out_map))
        side_shapes.append(jax.ShapeDtypeStruct((rows, cols), BF16))
    if fuse_norm:
        lhs_specs = [pl.BlockSpec((None, tm, d), lambda n, i: (0, i, 0)),
                     pl.BlockSpec((1, d), lambda n, i: (0, 0))]
        lhs_args = (lhs, gain)
        extra_specs = [pl.BlockSpec((tm, d), lambda n, i: (i, 0))]
        extra_shapes = [jax.ShapeDtypeStruct((s, d), BF16)]
        w_mode = dict(pipeline_mode=pl.Buffered(1))
    else:
        lhs_specs = [pl.BlockSpec((tm, d), lambda n, i: (i, 0))]
        lhs_args = (lhs,)
        extra_specs, extra_shapes, w_mode = [], [], {}
    last_tile = tile0 + n_tiles - 1
    stride = n_i // n_wchunks
    assert stride * n_wchunks == n_i and d % n_wchunks == 0

    def w_map(n, i, k):
        ahead = jnp.where(i > k * stride, 1, 0)
        return (0, k, jnp.minimum(n + tile0 + ahead, last_tile))

    w_specs = [pl.BlockSpec((None, d // n_wchunks, tn), functools.partial(w_map, k=k), **w_mode)
               for k in range(n_wchunks)]
    const = lambda n, i: (0, 0)
    outs = pl.pallas_call(
        kern,
        grid=(n_tiles, n_i),
        in_specs=[pl.BlockSpec(memory_space=pltpu.SMEM)] + lhs_specs + w_specs + [
            pl.BlockSpec((tm, head_w), const),
            pl.BlockSpec((tm, head_w), const),
            pl.BlockSpec((None, 1, head_w), lambda n, i: (i, 0, 0)),
            pl.BlockSpec((None, 1, head_w), lambda n, i: (i, 0, 0))] + side_in_specs,
        out_specs=[pl.BlockSpec((tm, tn), lambda n, i: (i, n))] + extra_specs + side_specs,
        out_shape=[jax.ShapeDtypeStruct((s, n_tiles * tn), BF16)] + extra_shapes + side_shapes,
        scratch_shapes=[pltpu.VMEM((d, tn), BF16),
                        pltpu.VMEM((tm, head_w), F32),
                        pltpu.VMEM((tm, head_w), F32)],
        compiler_params=_cparams(2),
        name="ret_inproj_first" if fuse_norm else "ret_inproj",
    )(log_gamma, *lhs_args, *([w_in] * n_wchunks), cr, sr, ct, st,
      *[sw[0] for sw in side_weights])
    return outs[0], outs[1:]


def _ret_outproj_kernel(lg_ref, first_ref, rest_ref, x_ref, wout_ref, o_ref, state_ref, dmat_ref,
                        *, heads, dk, dv):
    t = pl.program_id(0)
    cb, n_first = first_ref.shape

    def cols(start, width):
        if start + width <= n_first:
            return first_ref[:, start:start + width]
        assert start >= n_first
        return rest_ref[:, start - n_first:start - n_first + width]

    @pl.when(t == 0)
    def _():
        state_ref[...] = jnp.zeros_like(state_ref)
        r = lax.broadcasted_iota(jnp.int32, (cb, cb), 0)
        c = lax.broadcasted_iota(jnp.int32, (cb, cb), 1)
        for h in range(heads):
            undo = jnp.exp(jnp.full((cb, cb), -cb, F32) * lg_ref[h])
            dmat_ref[h] = jnp.where(r >= c, undo, 0.0)

    k0, v0, g0 = heads * dk, 2 * heads * dk, 2 * heads * dk + heads * dv

    def mix(h):
        qd = cols(h * dk, dk)
        kd = cols(k0 + h * dk, dk)
        v = cols(v0 + h * dv, dv)
        block_decay = jnp.exp(jnp.full((1, 1), cb, F32) * lg_ref[h])
        scores = lax.dot_general(qd, kd, (((1,), (1,)), ((), ())), preferred_element_type=F32)
        state = state_ref[h]
        cross = jnp.dot(qd, state.astype(BF16), preferred_element_type=F32)
        update = lax.dot_general(kd, v, (((0,), (0,)), ((), ())), preferred_element_type=F32)
        p = (scores * dmat_ref[h]).astype(BF16)
        y = cross + jnp.dot(p, v, preferred_element_type=F32)
        state_ref[h] = state * block_decay + update
        return y

    def norm_gate(h, y):
        gate = cols(g0 + h * dv, dv)
        mu = jnp.mean(y, axis=-1, keepdims=True)
        yc = y - mu
        var = jnp.mean(yc * yc, axis=-1, keepdims=True)
        return (yc * lax.rsqrt(var + GN_EPS)).astype(BF16) * gate

    def project(h, yg, acc):
        return acc + jnp.dot(yg, wout_ref[h * dv:(h + 1) * dv, :], preferred_element_type=F32)

    acc = x_ref[...]
    ys, ygs = {}, {}
    for step in range(heads + 2):
        if 0 <= step - 2 < heads:
            acc = project(step - 2, ygs.pop(step - 2), acc)
        if step < heads:
            ys[step] = mix(step)
        if 0 <= step - 1 < heads:
            ygs[step - 1] = norm_gate(step - 1, ys.pop(step - 1))
    o_ref[...] = acc


def _ret_outproj(qkvg_first, qkvg_rest, x, w_out, log_gamma, heads, dk, dv, cb=RET_BLOCK):
    s, n_first = qkvg_first.shape
    n_rest = qkvg_rest.shape[1]
    d = w_out.shape[1]
    return pl.pallas_call(
        functools.partial(_ret_outproj_kernel, heads=heads, dk=dk, dv=dv),
        grid=(s // cb,),
        in_specs=[pl.BlockSpec(memory_space=pltpu.SMEM),
                  pl.BlockSpec((cb, n_first), lambda t: (t, 0)),
                  pl.BlockSpec((cb, n_rest), lambda t: (t, 0)),
                  pl.BlockSpec((None, cb, d), lambda t: (0, t, 0)),
                  pl.BlockSpec((heads * dv, d), lambda t: (0, 0), pipeline_mode=pl.Buffered(1))],
        out_specs=pl.BlockSpec((cb, d), lambda t: (t, 0)),
        out_shape=jax.ShapeDtypeStruct((s, d), F32),
        scratch_shapes=[pltpu.VMEM((heads, dk, dv), F32), pltpu.VMEM((heads, cb, cb), F32)],
        compiler_params=_cparams(1),
        name="ret_outproj",
    )(log_gamma, qkvg_first, qkvg_rest, x, w_out)


def _ple_kernel(h_ref, p_ref, pn_ref, wgate_ref, wproj_ref, nn_ref, h_out_ref, xn_out_ref, *,
                sub):
    for r in range(h_ref.shape[0] // sub):
        rows = slice(r * sub, (r + 1) * sub)
        h1 = h_ref[rows, :]
        xn = _rms(h1, pn_ref[...]).astype(BF16)
        gate = _sigmoid(jnp.dot(xn, wgate_ref[...], preferred_element_type=F32))
        emb = jnp.dot(p_ref[rows, :].astype(BF16), wproj_ref[...], preferred_element_type=F32)
        h2 = h1 + gate * emb
        h_out_ref[rows, :] = h2
        xn_out_ref[rows, :] = _rms(h2, nn_ref[...]).astype(BF16)


def _ple(h, p, layer, ple_norm, w_gate, w_proj, next_norm, tm=512, sub=256):
    s, d = h.shape
    pd = p.shape[-1]
    row = lambda i: (i, 0)
    resident = lambda shape, idx: pl.BlockSpec(shape, lambda i: idx, pipeline_mode=pl.Buffered(1))
    return pl.pallas_call(
        functools.partial(_ple_kernel, sub=sub),
        grid=(s // tm,),
        in_specs=[pl.BlockSpec((tm, d), row),
                  pl.BlockSpec((None, None, tm, pd), lambda i: (layer, 0, i, 0)),
                  resident((None, 1, d), (layer, 0, 0)),
                  resident((None, d, d), (layer, 0, 0)),
                  resident((None, pd, d), (layer, 0, 0)),
                  resident((1, d), (0, 0))],
        out_specs=(pl.BlockSpec((tm, d), row), pl.BlockSpec((tm, d), row)),
        out_shape=(jax.ShapeDtypeStruct((s, d), F32), jax.ShapeDtypeStruct((s, d), BF16)),
        compiler_params=_cparams(1),
        name="ple",
    )(h, p, ple_norm, w_gate, w_proj, next_norm)


def _outproj_ple_final_kernel(z_ref, h_ref, p_ref, wout_ref, pn_ref, wgate_ref, wproj_ref,
                              fn_ref, o_ref):
    h1 = h_ref[...] + jnp.dot(z_ref[...], wout_ref[...], preferred_element_type=F32)
    xn = _rms(h1, pn_ref[...]).astype(BF16)
    gate = _sigmoid(jnp.dot(xn, wgate_ref[...], preferred_element_type=F32))
    emb = jnp.dot(p_ref[...].astype(BF16), wproj_ref[...], preferred_element_type=F32)
    o_ref[...] = _rms(h1 + gate * emb, fn_ref[...])


def _outproj_ple_final(z, h, p, layer, w_out, ple_norm, w_gate, w_proj, final_norm, tm=256):
    s, d = h.shape
    kz = z.shape[1]
    pd = p.shape[-1]
    row = lambda i: (i, 0)
    resident = lambda shape, idx: pl.BlockSpec(shape, lambda i: idx, pipeline_mode=pl.Buffered(1))
    return pl.pallas_call(
        _outproj_ple_final_kernel,
        grid=(s // tm,),
        in_specs=[pl.BlockSpec((tm, kz), row),
                  pl.BlockSpec((tm, d), row),
                  pl.BlockSpec((None, None, tm, pd), lambda i: (layer, 0, i, 0)),
                  resident((kz, d), (0, 0)),
                  resident((None, 1, d), (layer, 0, 0)),
                  resident((None, d, d), (layer, 0, 0)),
                  resident((None, pd, d), (layer, 0, 0)),
                  resident((1, d), (0, 0))],
        out_specs=pl.BlockSpec((None, tm, d), lambda i: (0, i, 0)),
        out_shape=jax.ShapeDtypeStruct((1, s, d), F32),
        compiler_params=_cparams(1),
        name="outproj_ple_final",
    )(z, h, p, w_out, ple_norm, w_gate, w_proj, final_norm)


def _pool_wfuse_kernel(wu_ref, wgrp_ref, o_ref):
    o_ref[...] = jnp.dot(wu_ref[...].astype(BF16), wgrp_ref[...].astype(BF16),
                         preferred_element_type=F32).astype(o_ref.dtype)


def _pool_wfuse(w_in, w_group):
    _, d, _ = w_in.shape
    _, n_groups, gd, _ = w_group.shape
    return pl.pallas_call(
        _pool_wfuse_kernel,
        grid=(n_groups,),
        in_specs=[pl.BlockSpec((None, d, gd), lambda g: (0, 0, g)),
                  pl.BlockSpec((None, None, gd, gd), lambda g: (0, g, 0, 0))],
        out_specs=pl.BlockSpec((d, gd), lambda g: (0, g)),
        out_shape=jax.ShapeDtypeStruct((d, n_groups * gd), BF16),
        compiler_params=_cparams(1),
        name="pool_wfuse",
    )(w_in, w_group)


def _pool_kernel(xn_ref, wf_ref, wg_ref, scale_ref, o_ref, xbuf_ref, s2_ref, s4_ref, s8_ref, *,
                 windows):
    i = pl.program_id(0)
    tm = xn_ref.shape[0]
    gd = o_ref.shape[1] // len(windows)
    hl = POOL_HALO
    end = hl + tm

    @pl.when(i == 0)
    def _():
        xbuf_ref[0:hl, :] = jnp.zeros((hl, xbuf_ref.shape[1]), F32)

    xn = xn_ref[...]
    x32 = xn.astype(F32)
    xbuf_ref[hl:end, :] = x32
    s2_ref[8:end, :] = xbuf_ref[8:end, :] + xbuf_ref[7:end - 1, :]
    s4_ref[16:end, :] = s2_ref[16:end, :] + s2_ref[14:end - 2, :]
    s8_ref[24:end, :] = s4_ref[24:end, :] + s4_ref[20:end - 4, :]
    wsums = {2: s2_ref[hl:end, :], 4: s4_ref[hl:end, :], 8: s8_ref[hl:end, :],
             16: s8_ref[hl:end, :] + s8_ref[hl - 8:end - 8, :]}
    xbuf_ref[0:hl, :] = xbuf_ref[tm:end, :]

    tok = i * tm + lax.broadcasted_iota(jnp.int32, (tm, 1), 0)
    for g, w in enumerate(windows):
        cols = slice(g * gd, (g + 1) * gd)
        gate = jnp.dot(xn, wg_ref[:, cols], preferred_element_type=F32)
        gate = gate * _sigmoid(gate)
        inv_cnt = 1.0 / jnp.minimum(tok + 1, w).astype(F32)
        dev = (wsums[w] * inv_cnt - x32).astype(BF16)
        mixed = jnp.dot(dev, wf_ref[:, cols], preferred_element_type=F32) * scale_ref[:, cols]
        o_ref[:, cols] = (mixed * gate).astype(o_ref.dtype)


def _pool_mixer(xn, w_fused, w_gate, scale, tm=256):
    s, d = xn.shape
    width = w_fused.shape[1]
    assert set(POOL_WINDOWS) == {2, 4, 8, 16} and POOL_HALO >= 2 * max(POOL_WINDOWS)
    halo_buf = pltpu.VMEM((POOL_HALO + tm, d), F32)
    resident = lambda shape: pl.BlockSpec(shape, lambda i: (0, 0), pipeline_mode=pl.Buffered(1))
    return pl.pallas_call(
        functools.partial(_pool_kernel, windows=POOL_WINDOWS),
        grid=(s // tm,),
        in_specs=[pl.BlockSpec((tm, d), lambda i: (i, 0)),
                  resident((d, width)),
                  resident((d, width)),
                  resident((1, width))],
        out_specs=pl.BlockSpec((tm, width), lambda i: (i, 0)),
        out_shape=jax.ShapeDtypeStruct((s, width), BF16),
        scratch_shapes=[halo_buf, halo_buf, halo_buf, halo_buf],
        compiler_params=_cparams(1),
        name="pool_mixer",
    )(xn, w_fused, w_gate, scale)


def _rotary_tables(seq, dk, tm):
    freq = ROPE_BASE ** (-jnp.linspace(0.0, 1.0, dk // 2, dtype=F32))
    ang_r = jnp.arange(tm, dtype=F32)[:, None] * freq[None, :]
    ang_t = (jnp.arange(seq // tm, dtype=F32) * tm)[:, None, None] * freq[None, None, :]
    return jnp.cos(ang_r), jnp.sin(ang_r), jnp.cos(ang_t), jnp.sin(ang_t)


def kernel(x, p, ret_norm, ret_w_in, ret_w_out, pool_norm, pool_w_in, pool_w_group, pool_scale,
           pool_w_out, ple_norm, ple_w_gate, ple_w_proj, final_norm):
    b, s, d = x.shape
    assert b == 1
    heads = RET_HEADS
    dk = d // heads
    dv = ret_w_out.shape[1] // heads
    assert ret_w_in.shape[2] == 2 * heads * dk + 2 * heads * dv
    tm_inproj = tn_inproj = 1024

    tabs = _rotary_tables(s, dk, tm_inproj)
    log_gamma = jnp.log1p(-(2.0 ** (-5.0 - jnp.arange(heads, dtype=F32))))
    ple_norm3 = ple_norm.reshape(ple_norm.shape[0], 1, d)
    n_layers, pd, _ = ple_w_proj.shape
    n_groups, gd = pool_w_group.shape[1:3]
    pool_width = n_groups * gd

    whole = (0, 1)
    side = [(ret_w_out.reshape(heads * dv, d), 64, whole),
            (pool_w_in.reshape(d, 2 * pool_width), 32, (1, 2)),
            (pool_w_out.reshape(pool_width, d), 64, whole),
            (ple_w_gate.reshape(n_layers * d, d), 64, whole),
            (ple_w_proj.reshape(n_layers * pd, d), 32, whole)]

    n_tiles = ret_w_in.shape[2] // tn_inproj
    widths = (heads, heads * dk, heads * dv, dk ** -0.5, tm_inproj)
    qkvg_first, (xn0,) = _ret_inproj(x, ret_w_in, tabs, log_gamma, [], *widths, 0, 1, gain=ret_norm)
    qkvg_rest, (w_ret_out, w_pool_gate, w_pool_out, w_gate, w_proj) = _ret_inproj(
        xn0, ret_w_in, tabs, log_gamma, side, *widths, 1, n_tiles - 1)
    w_gate = w_gate.reshape(n_layers, d, d)
    w_proj = w_proj.reshape(n_layers, pd, d)
    h1 = _ret_outproj(qkvg_first, qkvg_rest, x, w_ret_out, log_gamma, heads, dk, dv)
    h2, xn2 = _ple(h1, p, 0, ple_norm3, w_gate, w_proj, pool_norm)

    w_pool_fused = _pool_wfuse(pool_w_in, pool_w_group)
    z = _pool_mixer(xn2, w_pool_fused, w_pool_gate, pool_scale)
    return _outproj_ple_final(z, h2, p, 1, w_pool_out, ple_norm3, w_gate, w_proj,
                              final_norm.reshape(1, d))
```

```python
import functools

import jax
import jax.numpy as jnp
from jax import lax
from jax.experimental import pallas as pl
from jax.experimental.pallas import tpu as pltpu

F32 = jnp.float32
BF16 = jnp.bfloat16

RET_HEADS = 8
ROPE_BASE = 10000.0
POOL_WINDOWS = (2, 4, 8, 16)
NORM_EPS = 1e-6
GN_EPS = 1e-5

LANES = 128
MXU_WIDTH = 256
POOL_HALO = 32
RET_BLOCK = 256
VMEM_LIMIT = 56 * 1024 * 1024


def _cparams(n_axes):
    return pltpu.CompilerParams(
        dimension_semantics=("arbitrary",) * n_axes,
        vmem_limit_bytes=VMEM_LIMIT,
    )


def _rms(x, gain):
    ms = jnp.mean(x * x, axis=-1, keepdims=True)
    return x * lax.rsqrt(ms + NORM_EPS) * gain


def _sigmoid(x):
    return 0.5 * jnp.tanh(0.5 * x) + 0.5


def _silu(x):
    h = 0.5 * x
    return h * jnp.tanh(h) + h


def _ret_inproj_kernel(lg_ref, lhs_ref, *rest, fuse_norm, n_wchunks, n_side, tile0, n_tiles,
                       heads, n_q_tiles, n_rot_tiles, n_plain_tiles, k_scale, cb):
    if fuse_norm:
        gain_ref, rest = rest[0], rest[1:]
    w_refs, rest = rest[:n_wchunks], rest[n_wchunks:]
    cr_ref, sr_ref, ct_ref, st_ref = rest[:4]
    rest = rest[4:]
    side_in, rest = rest[:n_side], rest[n_side:]
    o_ref, rest = rest[0], rest[1:]
    if fuse_norm:
        xn_ref, rest = rest[0], rest[1:]
    else:
        xn_ref = lhs_ref
    side_out, rest = rest[:n_side], rest[n_side:]
    wbf_ref, cs_ref, sn_ref = rest
    n = tile0 if n_tiles == 1 else pl.program_id(0) + tile0
    i = pl.program_id(1)
    tm, tn = o_ref.shape
    heads_per_tile = tn // MXU_WIDTH
    wrows = wbf_ref.shape[0] // n_wchunks

    def when(cond):
        if isinstance(cond, bool):
            return lambda f: f() if cond else None
        return pl.when(cond)

    @when(n >= n_rot_tiles)
    def _():
        @pl.when(i == 0)
        def _():
            for k, w_ref in enumerate(w_refs):
                wbf_ref[k * wrows:(k + 1) * wrows, :] = w_ref[...].astype(BF16)

    @when(n < n_rot_tiles)
    def _():
        @pl.when(i == 0)
        def _():
            src = lax.broadcasted_iota(jnp.int32, (MXU_WIDTH, MXU_WIDTH), 0)
            dst = lax.broadcasted_iota(jnp.int32, (MXU_WIDTH, MXU_WIDTH), 1)
            half = MXU_WIDTH // 2
            wanted = jnp.where(dst < half, 2 * dst, 2 * (dst - half) + 1)
            perm = (src == wanted).astype(BF16)
            for k, w_ref in enumerate(w_refs):
                for c in range(heads_per_tile):
                    cols = slice(c * MXU_WIDTH, (c + 1) * MXU_WIDTH)
                    wbf_ref[k * wrows:(k + 1) * wrows, cols] = jnp.dot(
                        w_ref[:, cols].astype(BF16), perm,
                        preferred_element_type=F32).astype(BF16)

    if fuse_norm:
        xn_ref[...] = _rms(lhs_ref[...], gain_ref[...]).astype(xn_ref.dtype)

    def run(epilogue):
        for src, dst in zip(side_in, side_out):
            dst[...] = src[...].astype(BF16)
        for c in range(heads_per_tile):
            cols = slice(c * MXU_WIDTH, (c + 1) * MXU_WIDTH)
            acc = jnp.dot(xn_ref[...], wbf_ref[:, cols], preferred_element_type=F32)
            o_ref[:, cols] = epilogue(acc, c).astype(o_ref.dtype)

    @when(n < n_rot_tiles)
    def _():
        ct, st = ct_ref[...], st_ref[...]
        cr, sr = cr_ref[...], sr_ref[...]
        cs_ref[...] = ct * cr - st * sr
        sn_ref[...] = st * cr + ct * sr
        is_k = n >= n_q_tiles
        row = lax.broadcasted_iota(jnp.int32, (tm, LANES), 0)
        j = (row & (cb - 1)).astype(F32)
        expo = jnp.where(is_k, cb - 1.0 - j, j + 1.0)
        scale = jnp.where(is_k, k_scale, 1.0).astype(F32)

        def rotate(acc, c):
            head = (n * heads_per_tile + c) % heads
            decay = jnp.exp(expo * lg_ref[head]) * scale
            cd = cs_ref[...] * decay
            sd = sn_ref[...] * decay
            xe, xo = acc[:, :LANES], acc[:, LANES:]
            return jnp.concatenate([xe * cd - xo * sd, xo * cd + xe * sd], axis=1)

        run(rotate)

    @when((n >= n_rot_tiles) & (n < n_plain_tiles))
    def _():
        run(lambda acc, c: acc)

    @when(n >= n_plain_tiles)
    def _():
        run(lambda acc, c: _silu(acc))


def _ret_inproj(lhs, w_in, tabs, log_gamma, side_weights, heads, qk_width, v_width, k_scale, tm,
                tile0, n_tiles, gain=None, tn=1024, cb=RET_BLOCK, n_wchunks=2):
    fuse_norm = gain is not None
    assert not (fuse_norm and (side_weights or n_tiles != 1))
    s, d = lhs.shape[-2:]
    cr, sr, ct, st = tabs
    head_w = cr.shape[1]
    assert 2 * head_w == MXU_WIDTH == 2 * LANES and cr.shape[0] == tm
    assert tm % cb == 0 and cb & (cb - 1) == 0
    n_i = s // tm
    n_rot_tiles = 2 * qk_width // tn
    kern = functools.partial(_ret_inproj_kernel, fuse_norm=fuse_norm, n_wchunks=n_wchunks,
                             n_side=len(side_weights),
                             tile0=tile0, n_tiles=n_tiles, heads=heads, n_q_tiles=qk_width // tn,
                             n_rot_tiles=n_rot_tiles, n_plain_tiles=n_rot_tiles + v_width // tn,
                             k_scale=k_scale, cb=cb)
    side_in_specs, side_specs, side_shapes = [], [], []
    for arr, chunk_rows, (col_blk, n_col_blks) in side_weights:
        rows, cols = arr.shape[0], arr.shape[1] // n_col_blks
        n_chunks = rows // chunk_rows
        assert n_chunks * chunk_rows == rows and n_chunks <= n_tiles * n_i
        in_map = lambda n, i, last=n_chunks - 1, cb_=col_blk: (jnp.minimum(n * n_i + i, last), cb_)
        out_map = lambda n, i, last=n_chunks - 1: (jnp.minimum(n * n_i + i, last), 0)
        side_in_specs.append(pl.BlockSpec((chunk_rows, cols), in_map))
        side_specs.append(pl.BlockSpec((chunk_rows, cols), out_map))
        side_shapes.append(jax.ShapeDtypeStruct((rows, cols), BF16))
    if fuse_norm:
        lhs_specs = [pl.BlockSpec((None, tm, d), lambda n, i: (0, i, 0)),
                     pl.BlockSpec((1, d), lambda n, i: (0, 0))]
        lhs_args = (lhs, gain)
        extra_specs = [pl.BlockSpec((tm, d), lambda n, i: (i, 0))]
        extra_shapes = [jax.ShapeDtypeStruct((s, d), BF16)]
        w_mode = dict(pipeline_mode=pl.Buffered(1))
    else:
        lhs_specs = [pl.BlockSpec((tm, d), lambda n, i: (i, 0))]
        lhs_args = (lhs,)
        extra_specs, extra_shapes, w_mode = [], [], {}
    last_tile = tile0 + n_tiles - 1
    stride = n_i // n_wchunks
    assert stride * n_wchunks == n_i and d % n_wchunks == 0

    def w_map(n, i, k):
        ahead = jnp.where(i > k * stride, 1, 0)
        return (0, k, jnp.minimum(n + tile0 + ahead, last_tile))

    w_specs = [pl.BlockSpec((None, d // n_wchunks, tn), functools.partial(w_map, k=k), **w_mode)
               for k in range(n_wchunks)]
    const = lambda n, i: (0, 0)
    outs = pl.pallas_call(
        kern,
        grid=(n_tiles, n_i),
        in_specs=[pl.BlockSpec(memory_space=pltpu.SMEM)] + lhs_specs + w_specs + [
            pl.BlockSpec((tm, head_w), const),
            pl.BlockSpec((tm, head_w), const),
            pl.BlockSpec((None, 1, head_w), lambda n, i: (i, 0, 0)),
            pl.BlockSpec((None, 1, head_w), lambda n, i: (i, 0, 0))] + side_in_specs,
        out_specs=[pl.BlockSpec((tm, tn), lambda n, i: (i, n))] + extra_specs + side_specs,
        out_shape=[jax.ShapeDtypeStruct((s, n_tiles * tn), BF16)] + extra_shapes + side_shapes,
        scratch_shapes=[pltpu.VMEM((d, tn), BF16),
                        pltpu.VMEM((tm, head_w), F32),
                        pltpu.VMEM((tm, head_w), F32)],
        compiler_params=_cparams(2),
        name="ret_inproj_first" if fuse_norm else "ret_inproj",
    )(log_gamma, *lhs_args, *([w_in] * n_wchunks), cr, sr, ct, st,
      *[sw[0] for sw in side_weights])
    return outs[0], outs[1:]


def _ret_outproj_kernel(lg_ref, first_ref, rest_ref, x_ref, wout_ref, o_ref, state_ref, dmat_ref,
                        *, heads, dk, dv):
    t = pl.program_id(0)
    cb, n_first = first_ref.shape

    def cols(start, width):
        if start + width <= n_first:
            return first_ref[:, start:start + width]
        assert start >= n_first
        return rest_ref[:, start - n_first:start - n_first + width]

    @pl.when(t == 0)
    def _():
        state_ref[...] = jnp.zeros_like(state_ref)
        r = lax.broadcasted_iota(jnp.int32, (cb, cb), 0)
        c = lax.broadcasted_iota(jnp.int32, (cb, cb), 1)
        for h in range(heads):
            undo = jnp.exp(jnp.full((cb, cb), -cb, F32) * lg_ref[h])
            dmat_ref[h] = jnp.where(r >= c, undo, 0.0)

    k0, v0, g0 = heads * dk, 2 * heads * dk, 2 * heads * dk + heads * dv

    def mix(h):
        qd = cols(h * dk, dk)
        kd = cols(k0 + h * dk, dk)
        v = cols(v0 + h * dv, dv)
        block_decay = jnp.exp(jnp.full((1, 1), cb, F32) * lg_ref[h])
        scores = lax.dot_general(qd, kd, (((1,), (1,)), ((), ())), preferred_element_type=F32)
        state = state_ref[h]
        cross = jnp.dot(qd, state.astype(BF16), preferred_element_type=F32)
        update = lax.dot_general(kd, v, (((0,), (0,)), ((), ())), preferred_element_type=F32)
        p = (scores * dmat_ref[h]).astype(BF16)
        y = cross + jnp.dot(p, v, preferred_element_type=F32)
        state_ref[h] = state * block_decay + update
        return y

    def norm_gate(h, y):
        gate = cols(g0 + h * dv, dv)
        mu = jnp.mean(y, axis=-1, keepdims=True)
        yc = y - mu
        var = jnp.mean(yc * yc, axis=-1, keepdims=True)
        return (yc * lax.rsqrt(var + GN_EPS)).astype(BF16) * gate

    def project(h, yg, acc):
        return acc + jnp.dot(yg, wout_ref[h * dv:(h + 1) * dv, :], preferred_element_type=F32)

    acc = x_ref[...]
    ys, ygs = {}, {}
    for step in range(heads + 2):
        if 0 <= step - 2 < heads:
            acc = project(step - 2, ygs.pop(step - 2), acc)
        if step < heads:
            ys[step] = mix(step)
        if 0 <= step - 1 < heads:
            ygs[step - 1] = norm_gate(step - 1, ys.pop(step - 1))
    o_ref[...] = acc


def _ret_outproj(qkvg_first, qkvg_rest, x, w_out, log_gamma, heads, dk, dv, cb=RET_BLOCK):
    s, n_first = qkvg_first.shape
    n_rest = qkvg_rest.shape[1]
    d = w_out.shape[1]
    return pl.pallas_call(
        functools.partial(_ret_outproj_kernel, heads=heads, dk=dk, dv=dv),
        grid=(s // cb,),
        in_specs=[pl.BlockSpec(memory_space=pltpu.SMEM),
                  pl.BlockSpec((cb, n_first), lambda t: (t, 0)),
                  pl.BlockSpec((cb, n_rest), lambda t: (t, 0)),
                  pl.BlockSpec((None, cb, d), lambda t: (0, t, 0)),
                  pl.BlockSpec((heads * dv, d), lambda t: (0, 0), pipeline_mode=pl.Buffered(1))],
        out_specs=pl.BlockSpec((cb, d), lambda t: (t, 0)),
        out_shape=jax.ShapeDtypeStruct((s, d), F32),
        scratch_shapes=[pltpu.VMEM((heads, dk, dv), F32), pltpu.VMEM((heads, cb, cb), F32)],
        compiler_params=_cparams(1),
        name="ret_outproj",
    )(log_gamma, qkvg_first, qkvg_rest, x, w_out)


def _ple_kernel(h_ref, p_ref, pn_ref, wgate_ref, wproj_ref, nn_ref, h_out_ref, xn_out_ref, *,
                sub):
    for r in range(h_ref.shape[0] // sub):
        rows = slice(r * sub, (r + 1) * sub)
        h1 = h_ref[rows, :]
        xn = _rms(h1, pn_ref[...]).astype(BF16)
        gate = _sigmoid(jnp.dot(xn, wgate_ref[...], preferred_element_type=F32))
        emb = jnp.dot(p_ref[rows, :].astype(BF16), wproj_ref[...], preferred_element_type=F32)
        h2 = h1 + gate * emb
        h_out_ref[rows, :] = h2
        xn_out_ref[rows, :] = _rms(h2, nn_ref[...]).astype(BF16)


def _ple(h, p, layer, ple_norm, w_gate, w_proj, next_norm, tm=512, sub=256):
    s, d = h.shape
    pd = p.shape[-1]
    row = lambda i: (i, 0)
    resident = lambda shape, idx: pl.BlockSpec(shape, lambda i: idx, pipeline_mode=pl.Buffered(1))
    return pl.pallas_call(
        functools.partial(_ple_kernel, sub=sub),
        grid=(s // tm,),
        in_specs=[pl.BlockSpec((tm, d), row),
                  pl.BlockSpec((None, None, tm, pd), lambda i: (layer, 0, i, 0)),
                  resident((None, 1, d), (layer, 0, 0)),
                  resident((None, d, d), (layer, 0, 0)),
                  resident((None, pd, d), (layer, 0, 0)),
                  resident((1, d), (0, 0))],
        out_specs=(pl.BlockSpec((tm, d), row), pl.BlockSpec((tm, d), row)),
        out_shape=(jax.ShapeDtypeStruct((s, d), F32), jax.ShapeDtypeStruct((s, d), BF16)),
        compiler_params=_cparams(1),
        name="ple",
    )(h, p, ple_norm, w_gate, w_proj, next_norm)


def _outproj_ple_final_kernel(z_ref, h_ref, p_ref, wout_ref, pn_ref, wgate_ref, wproj_ref,
                              fn_ref, o_ref):
    h1 = h_ref[...] + jnp.dot(z_ref[...], wout_ref[...], preferred_element_type=F32)
    xn = _rms(h1, pn_ref[...]).astype(BF16)
    gate = _sigmoid(jnp.dot(xn, wgate_ref[...], preferred_element_type=F32))
    emb = jnp.dot(p_ref[...].astype(BF16), wproj_ref[...], preferred_element_type=F32)
    o_ref[...] = _rms(h1 + gate * emb, fn_ref[...])


def _outproj_ple_final(z, h, p, layer, w_out, ple_norm, w_gate, w_proj, final_norm, tm=256):
    s, d = h.shape
    kz = z.shape[1]
    pd = p.shape[-1]
    row = lambda i: (i, 0)
    resident = lambda shape, idx: pl.BlockSpec(shape, lambda i: idx, pipeline_mode=pl.Buffered(1))
    return pl.pallas_call(
        _outproj_ple_final_kernel,
        grid=(s // tm,),
        in_specs=[pl.BlockSpec((tm, kz), row),
                  pl.BlockSpec((tm, d), row),
                  pl.BlockSpec((None, None, tm, pd), lambda i: (layer, 0, i, 0)),
                  resident((kz, d), (0, 0)),
                  resident((None, 1, d), (layer, 0, 0)),
                  resident((None, d, d), (layer, 0, 0)),
                  resident((None, pd, d), (layer, 0, 0)),
                  resident((1, d), (0, 0))],
        out_specs=pl.BlockSpec((None, tm, d), lambda i: (0, i, 0)),
        out_shape=jax.ShapeDtypeStruct((1, s, d), F32),
        compiler_params=_cparams(1),
        name="outproj_ple_final",
    )(z, h, p, w_out, ple_norm, w_gate, w_proj, final_norm)


def _pool_wfuse_kernel(wu_ref, wgrp_ref, o_ref):
    o_ref[...] = jnp.dot(wu_ref[...].astype(BF16), wgrp_ref[...].astype(BF16),
                         preferred_element_type=F32).astype(o_ref.dtype)


def _pool_wfuse(w_in, w_group):
    _, d, _ = w_in.shape
    _, n_groups, gd, _ = w_group.shape
    return pl.pallas_call(
        _pool_wfuse_kernel,
        grid=(n_groups,),
        in_specs=[pl.BlockSpec((None, d, gd), lambda g: (0, 0, g)),
                  pl.BlockSpec((None, None, gd, gd), lambda g: (0, g, 0, 0))],
        out_specs=pl.BlockSpec((d, gd), lambda g: (0, g)),
        out_shape=jax.ShapeDtypeStruct((d, n_groups * gd), BF16),
        compiler_params=_cparams(1),
        name="pool_wfuse",
    )(w_in, w_group)


def _pool_kernel(xn_ref, wf_ref, wg_ref, scale_ref, o_ref, xbuf_ref, s2_ref, s4_ref, s8_ref, *,
                 windows):
    i = pl.program_id(0)
    tm = xn_ref.shape[0]
    gd = o_ref.shape[1] // len(windows)
    hl = POOL_HALO
    end = hl + tm

    @pl.when(i == 0)
    def _():
        xbuf_ref[0:hl, :] = jnp.zeros((hl, xbuf_ref.shape[1]), F32)

    xn = xn_ref[...]
    x32 = xn.astype(F32)
    xbuf_ref[hl:end, :] = x32
    s2_ref[8:end, :] = xbuf_ref[8:end, :] + xbuf_ref[7:end - 1, :]
    s4_ref[16:end, :] = s2_ref[16:end, :] + s2_ref[14:end - 2, :]
    s8_ref[24:end, :] = s4_ref[24:end, :] + s4_ref[20:end - 4, :]
    wsums = {2: s2_ref[hl:end, :], 4: s4_ref[hl:end, :], 8: s8_ref[hl:end, :],
             16: s8_ref[hl:end, :] + s8_ref[hl - 8:end - 8, :]}
    xbuf_ref[0:hl, :] = xbuf_ref[tm:end, :]

    tok = i * tm + lax.broadcasted_iota(jnp.int32, (tm, 1), 0)
    for g, w in enumerate(windows):
        cols = slice(g * gd, (g + 1) * gd)
        gate = jnp.dot(xn, wg_ref[:, cols], preferred_element_type=F32)
        gate = gate * _sigmoid(gate)
        inv_cnt = 1.0 / jnp.minimum(tok + 1, w).astype(F32)
        dev = (wsums[w] * inv_cnt - x32).astype(BF16)
        mixed = jnp.dot(dev, wf_ref[:, cols], preferred_element_type=F32) * scale_ref[:, cols]
        o_ref[:, cols] = (mixed * gate).astype(o_ref.dtype)


def _pool_mixer(xn, w_fused, w_gate, scale, tm=256):
    s, d = xn.shape
    width = w_fused.shape[1]
    assert set(POOL_WINDOWS) == {2, 4, 8, 16} and POOL_HALO >= 2 * max(POOL_WINDOWS)
    halo_buf = pltpu.VMEM((POOL_HALO + tm, d), F32)
    resident = lambda shape: pl.BlockSpec(shape, lambda i: (0, 0), pipeline_mode=pl.Buffered(1))
    return pl.pallas_call(
        functools.partial(_pool_kernel, windows=POOL_WINDOWS),
        grid=(s // tm,),
        in_specs=[pl.BlockSpec((tm, d), lambda i: (i, 0)),
                  resident((d, width)),
                  resident((d, width)),
                  resident((1, width))],
        out_specs=pl.BlockSpec((tm, width), lambda i: (i, 0)),
        out_shape=jax.ShapeDtypeStruct((s, width), BF16),
        scratch_shapes=[halo_buf, halo_buf, halo_buf, halo_buf],
        compiler_params=_cparams(1),
        name="pool_mixer",
    )(xn, w_fused, w_gate, scale)


def _rotary_tables(seq, dk, tm):
    freq = ROPE_BASE ** (-jnp.linspace(0.0, 1.0, dk // 2, dtype=F32))
    ang_r = jnp.arange(tm, dtype=F32)[:, None] * freq[None, :]
    ang_t = (jnp.arange(seq // tm, dtype=F32) * tm)[:, None, None] * freq[None, None, :]
    return jnp.cos(ang_r), jnp.sin(ang_r), jnp.cos(ang_t), jnp.sin(ang_t)


def kernel(x, p, ret_norm, ret_w_in, ret_w_out, pool_norm, pool_w_in, pool_w_group, pool_scale,
           pool_w_out, ple_norm, ple_w_gate, ple_w_proj, final_norm):
    b, s, d = x.shape
    assert b == 1
    heads = RET_HEADS
    dk = d // heads
    dv = ret_w_out.shape[1] // heads
    assert ret_w_in.shape[2] == 2 * heads * dk + 2 * heads * dv
    tm_inproj = tn_inproj = 1024

    tabs = _rotary_tables(s, dk, tm_inproj)
    log_gamma = jnp.log1p(-(2.0 ** (-5.0 - jnp.arange(heads, dtype=F32))))
    ple_norm3 = ple_norm.reshape(ple_norm.shape[0], 1, d)
    n_layers, pd, _ = ple_w_proj.shape
    n_groups, gd = pool_w_group.shape[1:3]
    pool_width = n_groups * gd

    whole = (0, 1)
    side = [(ret_w_out.reshape(heads * dv, d), 64, whole),
            (pool_w_in.reshape(d, 2 * pool_width), 32, (1, 2)),
            (pool_w_out.reshape(pool_width, d), 64, whole),
            (ple_w_gate.reshape(n_layers * d, d), 64, whole),
            (ple_w_proj.reshape(n_layers * pd, d), 32, whole)]

    n_tiles = ret_w_in.shape[2] // tn_inproj
    widths = (heads, heads * dk, heads * dv, dk ** -0.5, tm_inproj)
    qkvg_first, (xn0,) = _ret_inproj(x, ret_w_in, tabs, log_gamma, [], *widths, 0, 1, gain=ret_norm,
                                     n_wchunks=1)
    qkvg_rest, (w_ret_out, w_pool_gate, w_pool_out, w_gate, w_proj) = _ret_inproj(
        xn0, ret_w_in, tabs, log_gamma, side, *widths, 1, n_tiles - 1)
    w_gate = w_gate.reshape(n_layers, d, d)
    w_proj = w_proj.reshape(n_layers, pd, d)
    h1 = _ret_outproj(qkvg_first, qkvg_rest, x, w_ret_out, log_gamma, heads, dk, dv)
    h2, xn2 = _ple(h1, p, 0, ple_norm3, w_gate, w_proj, pool_norm)

    w_pool_fused = _pool_wfuse(pool_w_in, pool_w_group)
    z = _pool_mixer(xn2, w_pool_fused, w_pool_gate, pool_scale)
    return _outproj_ple_final(z, h2, p, 1, w_pool_out, ple_norm3, w_gate, w_proj,
                              final_norm.reshape(1, d))
```

```python
import functools

import jax
import jax.numpy as jnp
from jax import lax
from jax.experimental import pallas as pl
from jax.experimental.pallas import tpu as pltpu

F32 = jnp.float32
BF16 = jnp.bfloat16

RET_HEADS = 8
ROPE_BASE = 10000.0
POOL_WINDOWS = (2, 4, 8, 16)
NORM_EPS = 1e-6
GN_EPS = 1e-5

LANES = 128
MXU_WIDTH = 256
POOL_HALO = 32
RET_BLOCK = 256
VMEM_LIMIT = 56 * 1024 * 1024


def _cparams(n_axes):
    return pltpu.CompilerParams(
        dimension_semantics=("arbitrary",) * n_axes,
        vmem_limit_bytes=VMEM_LIMIT,
    )


def _rms(x, gain):
    ms = jnp.mean(x * x, axis=-1, keepdims=True)
    return x * lax.rsqrt(ms + NORM_EPS) * gain


def _sigmoid(x):
    return 0.5 * jnp.tanh(0.5 * x) + 0.5


def _silu(x):
    h = 0.5 * x
    return h * jnp.tanh(h) + h


def _ret_inproj_kernel(lg_ref, lhs_ref, *rest, fuse_norm, n_wchunks, n_side, tile0, n_tiles,
                       heads, n_q_tiles, n_rot_tiles, n_plain_tiles, k_scale, cb):
    if fuse_norm:
        gain_ref, rest = rest[0], rest[1:]
    w_refs, rest = rest[:n_wchunks], rest[n_wchunks:]
    cr_ref, sr_ref, ct_ref, st_ref = rest[:4]
    rest = rest[4:]
    side_in, rest = rest[:n_side], rest[n_side:]
    o_ref, rest = rest[0], rest[1:]
    if fuse_norm:
        xn_ref, rest = rest[0], rest[1:]
    else:
        xn_ref = lhs_ref
    side_out, rest = rest[:n_side], rest[n_side:]
    wbf_ref, cs_ref, sn_ref = rest
    n = tile0 if n_tiles == 1 else pl.program_id(0) + tile0
    i = pl.program_id(1)
    tm, tn = o_ref.shape
    heads_per_tile = tn // MXU_WIDTH
    wrows = wbf_ref.shape[0] // n_wchunks

    def when(cond):
        if isinstance(cond, bool):
            return lambda f: f() if cond else None
        return pl.when(cond)

    @when(n >= n_rot_tiles)
    def _():
        @pl.when(i == 0)
        def _():
            for k, w_ref in enumerate(w_refs):
                wbf_ref[k * wrows:(k + 1) * wrows, :] = w_ref[...].astype(BF16)

    @when(n < n_rot_tiles)
    def _():
        @pl.when(i == 0)
        def _():
            src = lax.broadcasted_iota(jnp.int32, (MXU_WIDTH, MXU_WIDTH), 0)
            dst = lax.broadcasted_iota(jnp.int32, (MXU_WIDTH, MXU_WIDTH), 1)
            half = MXU_WIDTH // 2
            wanted = jnp.where(dst < half, 2 * dst, 2 * (dst - half) + 1)
            perm = (src == wanted).astype(BF16)
            for k, w_ref in enumerate(w_refs):
                for c in range(heads_per_tile):
                    cols = slice(c * MXU_WIDTH, (c + 1) * MXU_WIDTH)
                    wbf_ref[k * wrows:(k + 1) * wrows, cols] = jnp.dot(
                        w_ref[:, cols].astype(BF16), perm,
                        preferred_element_type=F32).astype(BF16)

    if fuse_norm:
        xn_ref[...] = _rms(lhs_ref[...], gain_ref[...]).astype(xn_ref.dtype)

    def run(epilogue):
        for src, dst in zip(side_in, side_out):
            dst[...] = src[...].astype(BF16)
        for c in range(heads_per_tile):
            cols = slice(c * MXU_WIDTH, (c + 1) * MXU_WIDTH)
            acc = jnp.dot(xn_ref[...], wbf_ref[:, cols], preferred_element_type=F32)
            o_ref[:, cols] = epilogue(acc, c).astype(o_ref.dtype)

    @when(n < n_rot_tiles)
    def _():
        ct, st = ct_ref[...], st_ref[...]
        cr, sr = cr_ref[...], sr_ref[...]
        cs_ref[...] = ct * cr - st * sr
        sn_ref[...] = st * cr + ct * sr
        is_k = n >= n_q_tiles
        row = lax.broadcasted_iota(jnp.int32, (tm, LANES), 0)
        j = (row & (cb - 1)).astype(F32)
        expo = jnp.where(is_k, cb - 1.0 - j, j + 1.0)
        scale = jnp.where(is_k, k_scale, 1.0).astype(F32)

        def rotate(acc, c):
            head = (n * heads_per_tile + c) % heads
            decay = jnp.exp(expo * lg_ref[head]) * scale
            cd = cs_ref[...] * decay
            sd = sn_ref[...] * decay
            xe, xo = acc[:, :LANES], acc[:, LANES:]
            return jnp.concatenate([xe * cd - xo * sd, xo * cd + xe * sd], axis=1)

        run(rotate)

    @when((n >= n_rot_tiles) & (n < n_plain_tiles))
    def _():
        run(lambda acc, c: acc)

    @when(n >= n_plain_tiles)
    def _():
        run(lambda acc, c: _silu(acc))


def _ret_inproj(lhs, w_in, tabs, log_gamma, side_weights, heads, qk_width, v_width, k_scale, tm,
                tile0, n_tiles, gain=None, tn=1024, cb=RET_BLOCK, n_wchunks=2):
    fuse_norm = gain is not None
    assert not (fuse_norm and (side_weights or n_tiles != 1))
    s, d = lhs.shape[-2:]
    cr, sr, ct, st = tabs
    head_w = cr.shape[1]
    assert 2 * head_w == MXU_WIDTH == 2 * LANES and cr.shape[0] == tm
    assert tm % cb == 0 and cb & (cb - 1) == 0
    n_i = s // tm
    n_rot_tiles = 2 * qk_width // tn
    kern = functools.partial(_ret_inproj_kernel, fuse_norm=fuse_norm, n_wchunks=n_wchunks,
                             n_side=len(side_weights),
                             tile0=tile0, n_tiles=n_tiles, heads=heads, n_q_tiles=qk_width // tn,
                             n_rot_tiles=n_rot_tiles, n_plain_tiles=n_rot_tiles + v_width // tn,
                             k_scale=k_scale, cb=cb)
    side_in_specs, side_specs, side_shapes = [], [], []
    for arr, chunk_rows, (col_blk, n_col_blks) in side_weights:
        rows, cols = arr.shape[0], arr.shape[1] // n_col_blks
        n_chunks = rows // chunk_rows
        assert n_chunks * chunk_rows == rows and n_chunks <= n_tiles * n_i
        in_map = lambda n, i, last=n_chunks - 1, cb_=col_blk: (jnp.minimum(n * n_i + i, last), cb_)
        out_map = lambda n, i, last=n_chunks - 1: (jnp.minimum(n * n_i + i, last), 0)
        side_in_specs.append(pl.BlockSpec((chunk_rows, cols), in_map))
        side_specs.append(pl.BlockSpec((chunk_rows, cols), out_map))
        side_shapes.append(jax.ShapeDtypeStruct((rows, cols), BF16))
    if fuse_norm:
        lhs_specs = [pl.BlockSpec((None, tm, d), lambda n, i: (0, i, 0)),
                     pl.BlockSpec((1, d), lambda n, i: (0, 0))]
        lhs_args = (lhs, gain)
        extra_specs = [pl.BlockSpec((tm, d), lambda n, i: (i, 0))]
        extra_shapes = [jax.ShapeDtypeStruct((s, d), BF16)]
        w_mode = dict(pipeline_mode=pl.Buffered(1))
    else:
        lhs_specs = [pl.BlockSpec((tm, d), lambda n, i: (i, 0))]
        lhs_args = (lhs,)
        extra_specs, extra_shapes, w_mode = [], [], {}
    last_tile = tile0 + n_tiles - 1
    stride = n_i // n_wchunks
    assert stride * n_wchunks == n_i and d % n_wchunks == 0

    def w_map(n, i, k):
        ahead = jnp.where(i > k * stride, 1, 0)
        return (0, k, jnp.minimum(n + tile0 + ahead, last_tile))

    w_specs = [pl.BlockSpec((None, d // n_wchunks, tn), functools.partial(w_map, k=k), **w_mode)
               for k in range(n_wchunks)]
    const = lambda n, i: (0, 0)
    outs = pl.pallas_call(
        kern,
        grid=(n_tiles, n_i),
        in_specs=[pl.BlockSpec(memory_space=pltpu.SMEM)] + lhs_specs + w_specs + [
            pl.BlockSpec((tm, head_w), const),
            pl.BlockSpec((tm, head_w), const),
            pl.BlockSpec((None, 1, head_w), lambda n, i: (i, 0, 0)),
            pl.BlockSpec((None, 1, head_w), lambda n, i: (i, 0, 0))] + side_in_specs,
        out_specs=[pl.BlockSpec((tm, tn), lambda n, i: (i, n))] + extra_specs + side_specs,
        out_shape=[jax.ShapeDtypeStruct((s, n_tiles * tn), BF16)] + extra_shapes + side_shapes,
        scratch_shapes=[pltpu.VMEM((d, tn), BF16),
                        pltpu.VMEM((tm, head_w), F32),
                        pltpu.VMEM((tm, head_w), F32)],
        compiler_params=_cparams(2),
        name="ret_inproj_first" if fuse_norm else "ret_inproj",
    )(log_gamma, *lhs_args, *([w_in] * n_wchunks), cr, sr, ct, st,
      *[sw[0] for sw in side_weights])
    return outs[0], outs[1:]


def _ret_outproj_kernel(lg_ref, first_ref, rest_ref, x_ref, wout_ref, o_ref, state_ref, dmat_ref,
                        *, heads, dk, dv):
    t = pl.program_id(0)
    cb, n_first = first_ref.shape

    def cols(start, width):
        if start + width <= n_first:
            return first_ref[:, start:start + width]
        assert start >= n_first
        return rest_ref[:, start - n_first:start - n_first + width]

    @pl.when(t == 0)
    def _():
        state_ref[...] = jnp.zeros_like(state_ref)
        r = lax.broadcasted_iota(jnp.int32, (cb, cb), 0)
        c = lax.broadcasted_iota(jnp.int32, (cb, cb), 1)
        for h in range(heads):
            undo = jnp.exp(jnp.full((cb, cb), -cb, F32) * lg_ref[h])
            dmat_ref[h] = jnp.where(r >= c, undo, 0.0)

    k0, v0, g0 = heads * dk, 2 * heads * dk, 2 * heads * dk + heads * dv

    def mix(h):
        qd = cols(h * dk, dk)
        kd = cols(k0 + h * dk, dk)
        v = cols(v0 + h * dv, dv)
        block_decay = jnp.exp(jnp.full((1, 1), cb, F32) * lg_ref[h])
        scores = lax.dot_general(qd, kd, (((1,), (1,)), ((), ())), preferred_element_type=F32)
        state = state_ref[h]
        cross = jnp.dot(qd, state.astype(BF16), preferred_element_type=F32)
        update = lax.dot_general(kd, v, (((0,), (0,)), ((), ())), preferred_element_type=F32)
        p = (scores * dmat_ref[h]).astype(BF16)
        y = cross + jnp.dot(p, v, preferred_element_type=F32)
        state_ref[h] = state * block_decay + update
        return y

    def norm_gate(h, y):
        gate = cols(g0 + h * dv, dv)
        mu = jnp.mean(y, axis=-1, keepdims=True)
        yc = y - mu
        var = jnp.mean(yc * yc, axis=-1, keepdims=True)
        return (yc * lax.rsqrt(var + GN_EPS)).astype(BF16) * gate

    def project(h, yg, acc):
        return acc + jnp.dot(yg, wout_ref[h * dv:(h + 1) * dv, :], preferred_element_type=F32)

    acc = x_ref[...]
    ys, ygs = {}, {}
    for step in range(heads + 2):
        if 0 <= step - 2 < heads:
            acc = project(step - 2, ygs.pop(step - 2), acc)
        if step < heads:
            ys[step] = mix(step)
        if 0 <= step - 1 < heads:
            ygs[step - 1] = norm_gate(step - 1, ys.pop(step - 1))
    o_ref[...] = acc


def _ret_outproj(qkvg_first, qkvg_rest, x, w_out, log_gamma, heads, dk, dv, cb=RET_BLOCK):
    s, n_first = qkvg_first.shape
    n_rest = qkvg_rest.shape[1]
    d = w_out.shape[1]
    return pl.pallas_call(
        functools.partial(_ret_outproj_kernel, heads=heads, dk=dk, dv=dv),
        grid=(s // cb,),
        in_specs=[pl.BlockSpec(memory_space=pltpu.SMEM),
                  pl.BlockSpec((cb, n_first), lambda t: (t, 0)),
                  pl.BlockSpec((cb, n_rest), lambda t: (t, 0)),
                  pl.BlockSpec((None, cb, d), lambda t: (0, t, 0)),
                  pl.BlockSpec((heads * dv, d), lambda t: (0, 0), pipeline_mode=pl.Buffered(1))],
        out_specs=pl.BlockSpec((cb, d), lambda t: (t, 0)),
        out_shape=jax.ShapeDtypeStruct((s, d), F32),
        scratch_shapes=[pltpu.VMEM((heads, dk, dv), F32), pltpu.VMEM((heads, cb, cb), F32)],
        compiler_params=_cparams(1),
        name="ret_outproj",
    )(log_gamma, qkvg_first, qkvg_rest, x, w_out)


def _ple_kernel(h_ref, p_ref, pn_ref, wgate_ref, wproj_ref, nn_ref, h_out_ref, xn_out_ref, *,
                sub):
    for r in range(h_ref.shape[0] // sub):
        rows = slice(r * sub, (r + 1) * sub)
        h1 = h_ref[rows, :]
        half_emb = 0.5 * jnp.dot(p_ref[rows, :].astype(BF16), wproj_ref[...],
                                 preferred_element_type=F32)
        base = h1 + half_emb
        xn = _rms(h1, pn_ref[...]).astype(BF16)
        logits = jnp.dot(xn, wgate_ref[...], preferred_element_type=F32)
        h2 = base + half_emb * jnp.tanh(0.5 * logits)
        h_out_ref[rows, :] = h2
        xn_out_ref[rows, :] = _rms(h2, nn_ref[...]).astype(BF16)


def _ple(h, p, layer, ple_norm, w_gate, w_proj, next_norm, tm=512, sub=256):
    s, d = h.shape
    pd = p.shape[-1]
    row = lambda i: (i, 0)
    resident = lambda shape, idx: pl.BlockSpec(shape, lambda i: idx, pipeline_mode=pl.Buffered(1))
    return pl.pallas_call(
        functools.partial(_ple_kernel, sub=sub),
        grid=(s // tm,),
        in_specs=[pl.BlockSpec((tm, d), row),
                  pl.BlockSpec((None, None, tm, pd), lambda i: (layer, 0, i, 0)),
                  resident((None, 1, d), (layer, 0, 0)),
                  resident((None, d, d), (layer, 0, 0)),
                  resident((None, pd, d), (layer, 0, 0)),
                  resident((1, d), (0, 0))],
        out_specs=(pl.BlockSpec((tm, d), row), pl.BlockSpec((tm, d), row)),
        out_shape=(jax.ShapeDtypeStruct((s, d), F32), jax.ShapeDtypeStruct((s, d), BF16)),
        compiler_params=_cparams(1),
        name="ple",
    )(h, p, ple_norm, w_gate, w_proj, next_norm)


def _outproj_ple_final_kernel(z_ref, h_ref, p_ref, wout_ref, pn_ref, wgate_ref, wproj_ref,
                              fn_ref, o_ref):
    h1 = h_ref[...] + jnp.dot(z_ref[...], wout_ref[...], preferred_element_type=F32)
    half_emb = 0.5 * jnp.dot(p_ref[...].astype(BF16), wproj_ref[...], preferred_element_type=F32)
    base = h1 + half_emb
    xn = _rms(h1, pn_ref[...]).astype(BF16)
    logits = jnp.dot(xn, wgate_ref[...], preferred_element_type=F32)
    o_ref[...] = _rms(base + half_emb * jnp.tanh(0.5 * logits), fn_ref[...])


def _outproj_ple_final(z, h, p, layer, w_out, ple_norm, w_gate, w_proj, final_norm, tm=256):
    s, d = h.shape
    kz = z.shape[1]
    pd = p.shape[-1]
    row = lambda i: (i, 0)
    resident = lambda shape, idx: pl.BlockSpec(shape, lambda i: idx, pipeline_mode=pl.Buffered(1))
    return pl.pallas_call(
        _outproj_ple_final_kernel,
        grid=(s // tm,),
        in_specs=[pl.BlockSpec((tm, kz), row),
                  pl.BlockSpec((tm, d), row),
                  pl.BlockSpec((None, None, tm, pd), lambda i: (layer, 0, i, 0)),
                  resident((kz, d), (0, 0)),
                  resident((None, 1, d), (layer, 0, 0)),
                  resident((None, d, d), (layer, 0, 0)),
                  resident((None, pd, d), (layer, 0, 0)),
                  resident((1, d), (0, 0))],
        out_specs=pl.BlockSpec((None, tm, d), lambda i: (0, i, 0)),
        out_shape=jax.ShapeDtypeStruct((1, s, d), F32),
        compiler_params=_cparams(1),
        name="outproj_ple_final",
    )(z, h, p, w_out, ple_norm, w_gate, w_proj, final_norm)


def _pool_wfuse_kernel(wu_ref, wgrp_ref, o_ref):
    o_ref[...] = jnp.dot(wu_ref[...].astype(BF16), wgrp_ref[...].astype(BF16),
                         preferred_element_type=F32).astype(o_ref.dtype)


def _pool_wfuse(w_in, w_group):
    _, d, _ = w_in.shape
    _, n_groups, gd, _ = w_group.shape
    return pl.pallas_call(
        _pool_wfuse_kernel,
        grid=(n_groups,),
        in_specs=[pl.BlockSpec((None, d, gd), lambda g: (0, 0, g)),
                  pl.BlockSpec((None, None, gd, gd), lambda g: (0, g, 0, 0))],
        out_specs=pl.BlockSpec((d, gd), lambda g: (0, g)),
        out_shape=jax.ShapeDtypeStruct((d, n_groups * gd), BF16),
        compiler_params=_cparams(1),
        name="pool_wfuse",
    )(w_in, w_group)


def _pool_kernel(xn_ref, wf_ref, wg_ref, scale_ref, o_ref, xbuf_ref, s2_ref, s4_ref, s8_ref, *,
                 windows):
    i = pl.program_id(0)
    tm = xn_ref.shape[0]
    gd = o_ref.shape[1] // len(windows)
    hl = POOL_HALO
    end = hl + tm

    @pl.when(i == 0)
    def _():
        xbuf_ref[0:hl, :] = jnp.zeros((hl, xbuf_ref.shape[1]), F32)

    xn = xn_ref[...]
    x32 = xn.astype(F32)
    xbuf_ref[hl:end, :] = x32
    s2_ref[8:end, :] = xbuf_ref[8:end, :] + xbuf_ref[7:end - 1, :]
    s4_ref[16:end, :] = s2_ref[16:end, :] + s2_ref[14:end - 2, :]
    s8_ref[24:end, :] = s4_ref[24:end, :] + s4_ref[20:end - 4, :]
    wsums = {2: s2_ref[hl:end, :], 4: s4_ref[hl:end, :], 8: s8_ref[hl:end, :],
             16: s8_ref[hl:end, :] + s8_ref[hl - 8:end - 8, :]}
    xbuf_ref[0:hl, :] = xbuf_ref[tm:end, :]

    tok = i * tm + lax.broadcasted_iota(jnp.int32, (tm, 1), 0)
    for g, w in enumerate(windows):
        cols = slice(g * gd, (g + 1) * gd)
        gate = jnp.dot(xn, wg_ref[:, cols], preferred_element_type=F32)
        gate = gate * _sigmoid(gate)
        inv_cnt = 1.0 / jnp.minimum(tok + 1, w).astype(F32)
        dev = (wsums[w] * inv_cnt - x32).astype(BF16)
        mixed = jnp.dot(dev, wf_ref[:, cols], preferred_element_type=F32) * scale_ref[:, cols]
        o_ref[:, cols] = (mixed * gate).astype(o_ref.dtype)


def _pool_mixer(xn, w_fused, w_gate, scale, tm=256):
    s, d = xn.shape
    width = w_fused.shape[1]
    assert set(POOL_WINDOWS) == {2, 4, 8, 16} and POOL_HALO >= 2 * max(POOL_WINDOWS)
    halo_buf = pltpu.VMEM((POOL_HALO + tm, d), F32)
    resident = lambda shape: pl.BlockSpec(shape, lambda i: (0, 0), pipeline_mode=pl.Buffered(1))
    return pl.pallas_call(
        functools.partial(_pool_kernel, windows=POOL_WINDOWS),
        grid=(s // tm,),
        in_specs=[pl.BlockSpec((tm, d), lambda i: (i, 0)),
                  resident((d, width)),
                  resident((d, width)),
                  resident((1, width))],
        out_specs=pl.BlockSpec((tm, width), lambda i: (i, 0)),
        out_shape=jax.ShapeDtypeStruct((s, width), BF16),
        scratch_shapes=[halo_buf, halo_buf, halo_buf, halo_buf],
        compiler_params=_cparams(1),
        name="pool_mixer",
    )(xn, w_fused, w_gate, scale)


def _rotary_tables(seq, dk, tm):
    freq = ROPE_BASE ** (-jnp.linspace(0.0, 1.0, dk // 2, dtype=F32))
    ang_r = jnp.arange(tm, dtype=F32)[:, None] * freq[None, :]
    ang_t = (jnp.arange(seq // tm, dtype=F32) * tm)[:, None, None] * freq[None, None, :]
    return jnp.cos(ang_r), jnp.sin(ang_r), jnp.cos(ang_t), jnp.sin(ang_t)


def kernel(x, p, ret_norm, ret_w_in, ret_w_out, pool_norm, pool_w_in, pool_w_group, pool_scale,
           pool_w_out, ple_norm, ple_w_gate, ple_w_proj, final_norm):
    b, s, d = x.shape
    assert b == 1
    heads = RET_HEADS
    dk = d // heads
    dv = ret_w_out.shape[1] // heads
    assert ret_w_in.shape[2] == 2 * heads * dk + 2 * heads * dv
    tm_inproj = tn_inproj = 1024

    tabs = _rotary_tables(s, dk, tm_inproj)
    log_gamma = jnp.log1p(-(2.0 ** (-5.0 - jnp.arange(heads, dtype=F32))))
    ple_norm3 = ple_norm.reshape(ple_norm.shape[0], 1, d)
    n_layers, pd, _ = ple_w_proj.shape
    n_groups, gd = pool_w_group.shape[1:3]
    pool_width = n_groups * gd

    whole = (0, 1)
    side = [(ret_w_out.reshape(heads * dv, d), 64, whole),
            (pool_w_in.reshape(d, 2 * pool_width), 32, (1, 2)),
            (pool_w_out.reshape(pool_width, d), 64, whole),
            (ple_w_gate.reshape(n_layers * d, d), 64, whole),
            (ple_w_proj.reshape(n_layers * pd, d), 32, whole)]

    n_tiles = ret_w_in.shape[2] // tn_inproj
    widths = (heads, heads * dk, heads * dv, dk ** -0.5, tm_inproj)
    qkvg_first, (xn0,) = _ret_inproj(x, ret_w_in, tabs, log_gamma, [], *widths, 0, 1, gain=ret_norm,
                                     n_wchunks=1)
    qkvg_rest, (w_ret_out, w_pool_gate, w_pool_out, w_gate, w_proj) = _ret_inproj(
        xn0, ret_w_in, tabs, log_gamma, side, *widths, 1, n_tiles - 1)
    w_gate = w_gate.reshape(n_layers, d, d)
    w_proj = w_proj.reshape(n_layers, pd, d)
    h1 = _ret_outproj(qkvg_first, qkvg_rest, x, w_ret_out, log_gamma, heads, dk, dv)
    h2, xn2 = _ple(h1, p, 0, ple_norm3, w_gate, w_proj, pool_norm)

    w_pool_fused = _pool_wfuse(pool_w_in, pool_w_group)
    z = _pool_mixer(xn2, w_pool_fused, w_pool_gate, pool_scale)
    return _outproj_ple_final(z, h2, p, 1, w_pool_out, ple_norm3, w_gate, w_proj,
                              final_norm.reshape(1, d))
```

```python
import functools

import jax
import jax.numpy as jnp
from jax import lax
from jax.experimental import pallas as pl
from jax.experimental.pallas import tpu as pltpu

F32 = jnp.float32
BF16 = jnp.bfloat16

RET_HEADS = 8
ROPE_BASE = 10000.0
POOL_WINDOWS = (2, 4, 8, 16)
NORM_EPS = 1e-6
GN_EPS = 1e-5

LANES = 128
MXU_WIDTH = 256
POOL_HALO = 32
RET_BLOCK = 256
VMEM_LIMIT = 56 * 1024 * 1024


def _cparams(n_axes):
    return pltpu.CompilerParams(
        dimension_semantics=("arbitrary",) * n_axes,
        vmem_limit_bytes=VMEM_LIMIT,
    )


def _rms(x, gain):
    ms = jnp.mean(x * x, axis=-1, keepdims=True)
    return x * lax.rsqrt(ms + NORM_EPS) * gain


def _sigmoid(x):
    return 0.5 * jnp.tanh(0.5 * x) + 0.5


def _silu(x):
    h = 0.5 * x
    return h * jnp.tanh(h) + h


def _ret_inproj_kernel(lg_ref, lhs_ref, *rest, fuse_norm, n_wchunks, n_side, tile0, n_tiles,
                       heads, n_q_tiles, n_rot_tiles, n_plain_tiles, k_scale, cb):
    if fuse_norm:
        gain_ref, rest = rest[0], rest[1:]
    w_refs, rest = rest[:n_wchunks], rest[n_wchunks:]
    cr_ref, sr_ref, ct_ref, st_ref = rest[:4]
    rest = rest[4:]
    side_in, rest = rest[:n_side], rest[n_side:]
    o_ref, rest = rest[0], rest[1:]
    if fuse_norm:
        xn_ref, rest = rest[0], rest[1:]
    else:
        xn_ref = lhs_ref
    side_out, rest = rest[:n_side], rest[n_side:]
    wbf_ref, cs_ref, sn_ref = rest
    n = tile0 if n_tiles == 1 else pl.program_id(0) + tile0
    i = pl.program_id(1)
    tm, tn = o_ref.shape
    heads_per_tile = tn // MXU_WIDTH
    wrows = wbf_ref.shape[0] // n_wchunks

    def when(cond):
        if isinstance(cond, bool):
            return lambda f: f() if cond else None
        return pl.when(cond)

    @when(n >= n_rot_tiles)
    def _():
        @pl.when(i == 0)
        def _():
            for k, w_ref in enumerate(w_refs):
                wbf_ref[k * wrows:(k + 1) * wrows, :] = w_ref[...].astype(BF16)

    @when(n < n_rot_tiles)
    def _():
        @pl.when(i == 0)
        def _():
            src = lax.broadcasted_iota(jnp.int32, (MXU_WIDTH, MXU_WIDTH), 0)
            dst = lax.broadcasted_iota(jnp.int32, (MXU_WIDTH, MXU_WIDTH), 1)
            half = MXU_WIDTH // 2
            wanted = jnp.where(dst < half, 2 * dst, 2 * (dst - half) + 1)
            perm = (src == wanted).astype(BF16)
            for k, w_ref in enumerate(w_refs):
                for c in range(heads_per_tile):
                    cols = slice(c * MXU_WIDTH, (c + 1) * MXU_WIDTH)
                    wbf_ref[k * wrows:(k + 1) * wrows, cols] = jnp.dot(
                        w_ref[:, cols].astype(BF16), perm,
                        preferred_element_type=F32).astype(BF16)

    if fuse_norm:
        xn_ref[...] = _rms(lhs_ref[...], gain_ref[...]).astype(xn_ref.dtype)

    def run(epilogue):
        for src, dst in zip(side_in, side_out):
            dst[...] = src[...].astype(BF16)
        for c in range(heads_per_tile):
            cols = slice(c * MXU_WIDTH, (c + 1) * MXU_WIDTH)
            acc = jnp.dot(xn_ref[...], wbf_ref[:, cols], preferred_element_type=F32)
            o_ref[:, cols] = epilogue(acc, c).astype(o_ref.dtype)

    @when(n < n_rot_tiles)
    def _():
        ct, st = ct_ref[...], st_ref[...]
        cr, sr = cr_ref[...], sr_ref[...]
        cs_ref[...] = ct * cr - st * sr
        sn_ref[...] = st * cr + ct * sr
        is_k = n >= n_q_tiles
        row = lax.broadcasted_iota(jnp.int32, (tm, LANES), 0)
        j = (row & (cb - 1)).astype(F32)
        expo = jnp.where(is_k, cb - 1.0 - j, j + 1.0)
        scale = jnp.where(is_k, k_scale, 1.0).astype(F32)

        def rotate(acc, c):
            head = (n * heads_per_tile + c) % heads
            decay = jnp.exp(expo * lg_ref[head]) * scale
            cd = cs_ref[...] * decay
            sd = sn_ref[...] * decay
            xe, xo = acc[:, :LANES], acc[:, LANES:]
            return jnp.concatenate([xe * cd - xo * sd, xo * cd + xe * sd], axis=1)

        run(rotate)

    @when((n >= n_rot_tiles) & (n < n_plain_tiles))
    def _():
        run(lambda acc, c: acc)

    @when(n >= n_plain_tiles)
    def _():
        run(lambda acc, c: _silu(acc))


def _ret_inproj(lhs, w_in, tabs, log_gamma, side_weights, heads, qk_width, v_width, k_scale, tm,
                tile0, n_tiles, gain=None, tn=1024, cb=RET_BLOCK, n_wchunks=2):
    fuse_norm = gain is not None
    assert not (fuse_norm and (side_weights or n_tiles != 1))
    s, d = lhs.shape[-2:]
    cr, sr, ct, st = tabs
    head_w = cr.shape[1]
    assert 2 * head_w == MXU_WIDTH == 2 * LANES and cr.shape[0] == tm
    assert tm % cb == 0 and cb & (cb - 1) == 0
    n_i = s // tm
    n_rot_tiles = 2 * qk_width // tn
    kern = functools.partial(_ret_inproj_kernel, fuse_norm=fuse_norm, n_wchunks=n_wchunks,
                             n_side=len(side_weights),
                             tile0=tile0, n_tiles=n_tiles, heads=heads, n_q_tiles=qk_width // tn,
                             n_rot_tiles=n_rot_tiles, n_plain_tiles=n_rot_tiles + v_width // tn,
                             k_scale=k_scale, cb=cb)
    side_in_specs, side_specs, side_shapes = [], [], []
    for arr, chunk_rows, (col_blk, n_col_blks) in side_weights:
        rows, cols = arr.shape[0], arr.shape[1] // n_col_blks
        n_chunks = rows // chunk_rows
        assert n_chunks * chunk_rows == rows and n_chunks <= n_tiles * n_i
        in_map = lambda n, i, last=n_chunks - 1, cb_=col_blk: (jnp.minimum(n * n_i + i, last), cb_)
        out_map = lambda n, i, last=n_chunks - 1: (jnp.minimum(n * n_i + i, last), 0)
        side_in_specs.append(pl.BlockSpec((chunk_rows, cols), in_map))
        side_specs.append(pl.BlockSpec((chunk_rows, cols), out_map))
        side_shapes.append(jax.ShapeDtypeStruct((rows, cols), BF16))
    if fuse_norm:
        lhs_specs = [pl.BlockSpec((None, tm, d), lambda n, i: (0, i, 0)),
                     pl.BlockSpec((1, d), lambda n, i: (0, 0))]
        lhs_args = (lhs, gain)
        extra_specs = [pl.BlockSpec((tm, d), lambda n, i: (i, 0))]
        extra_shapes = [jax.ShapeDtypeStruct((s, d), BF16)]
        w_mode = dict(pipeline_mode=pl.Buffered(1))
    else:
        lhs_specs = [pl.BlockSpec((tm, d), lambda n, i: (i, 0))]
        lhs_args = (lhs,)
        extra_specs, extra_shapes, w_mode = [], [], {}
    last_tile = tile0 + n_tiles - 1
    stride = n_i // n_wchunks
    assert stride * n_wchunks == n_i and d % n_wchunks == 0

    def w_map(n, i, k):
        ahead = jnp.where(i > k * stride, 1, 0)
        return (0, k, jnp.minimum(n + tile0 + ahead, last_tile))

    w_specs = [pl.BlockSpec((None, d // n_wchunks, tn), functools.partial(w_map, k=k), **w_mode)
               for k in range(n_wchunks)]
    const = lambda n, i: (0, 0)
    outs = pl.pallas_call(
        kern,
        grid=(n_tiles, n_i),
        in_specs=[pl.BlockSpec(memory_space=pltpu.SMEM)] + lhs_specs + w_specs + [
            pl.BlockSpec((tm, head_w), const),
            pl.BlockSpec((tm, head_w), const),
            pl.BlockSpec((None, 1, head_w), lambda n, i: (i, 0, 0)),
            pl.BlockSpec((None, 1, head_w), lambda n, i: (i, 0, 0))] + side_in_specs,
        out_specs=[pl.BlockSpec((tm, tn), lambda n, i: (i, n))] + extra_specs + side_specs,
        out_shape=[jax.ShapeDtypeStruct((s, n_tiles * tn), BF16)] + extra_shapes + side_shapes,
        scratch_shapes=[pltpu.VMEM((d, tn), BF16),
                        pltpu.VMEM((tm, head_w), F32),
                        pltpu.VMEM((tm, head_w), F32)],
        compiler_params=_cparams(2),
        name="ret_inproj_first" if fuse_norm else "ret_inproj",
    )(log_gamma, *lhs_args, *([w_in] * n_wchunks), cr, sr, ct, st,
      *[sw[0] for sw in side_weights])
    return outs[0], outs[1:]


def _ret_outproj_kernel(lg_ref, first_ref, rest_ref, x_ref, wout_ref, o_ref, state_ref, dmat_ref,
                        *, heads, dk, dv):
    t = pl.program_id(0)
    cb, n_first = first_ref.shape

    def cols(start, width):
        if start + width <= n_first:
            return first_ref[:, start:start + width]
        assert start >= n_first
        return rest_ref[:, start - n_first:start - n_first + width]

    @pl.when(t == 0)
    def _():
        state_ref[...] = jnp.zeros_like(state_ref)
        r = lax.broadcasted_iota(jnp.int32, (cb, cb), 0)
        c = lax.broadcasted_iota(jnp.int32, (cb, cb), 1)
        for h in range(heads):
            undo = jnp.exp(jnp.full((cb, cb), -cb, F32) * lg_ref[h])
            dmat_ref[h] = jnp.where(r >= c, undo, 0.0)

    k0, v0, g0 = heads * dk, 2 * heads * dk, 2 * heads * dk + heads * dv

    def mix(h):
        qd = cols(h * dk, dk)
        kd = cols(k0 + h * dk, dk)
        v = cols(v0 + h * dv, dv)
        block_decay = jnp.exp(jnp.full((1, 1), cb, F32) * lg_ref[h])
        scores = lax.dot_general(qd, kd, (((1,), (1,)), ((), ())), preferred_element_type=F32)
        state = state_ref[h]
        cross = jnp.dot(qd, state.astype(BF16), preferred_element_type=F32)
        update = lax.dot_general(kd, v, (((0,), (0,)), ((), ())), preferred_element_type=F32)
        p = (scores * dmat_ref[h]).astype(BF16)
        y = cross + jnp.dot(p, v, preferred_element_type=F32)
        state_ref[h] = state * block_decay + update
        return y

    def norm_gate(h, y):
        gate = cols(g0 + h * dv, dv)
        mu = jnp.mean(y, axis=-1, keepdims=True)
        yc = y - mu
        var = jnp.mean(yc * yc, axis=-1, keepdims=True)
        return (yc * lax.rsqrt(var + GN_EPS)).astype(BF16) * gate

    def project(h, yg, acc):
        return acc + jnp.dot(yg, wout_ref[h * dv:(h + 1) * dv, :], preferred_element_type=F32)

    acc = x_ref[...]
    ys, ygs = {}, {}
    for step in range(heads + 2):
        if 0 <= step - 2 < heads:
            acc = project(step - 2, ygs.pop(step - 2), acc)
        if step < heads:
            ys[step] = mix(step)
        if 0 <= step - 1 < heads:
            ygs[step - 1] = norm_gate(step - 1, ys.pop(step - 1))
    o_ref[...] = acc


def _ret_outproj(qkvg_first, qkvg_rest, x, w_out, log_gamma, heads, dk, dv, cb=RET_BLOCK):
    s, n_first = qkvg_first.shape
    n_rest = qkvg_rest.shape[1]
    d = w_out.shape[1]
    return pl.pallas_call(
        functools.partial(_ret_outproj_kernel, heads=heads, dk=dk, dv=dv),
        grid=(s // cb,),
        in_specs=[pl.BlockSpec(memory_space=pltpu.SMEM),
                  pl.BlockSpec((cb, n_first), lambda t: (t, 0)),
                  pl.BlockSpec((cb, n_rest), lambda t: (t, 0)),
                  pl.BlockSpec((None, cb, d), lambda t: (0, t, 0)),
                  pl.BlockSpec((heads * dv, d), lambda t: (0, 0), pipeline_mode=pl.Buffered(1))],
        out_specs=pl.BlockSpec((cb, d), lambda t: (t, 0)),
        out_shape=jax.ShapeDtypeStruct((s, d), F32),
        scratch_shapes=[pltpu.VMEM((heads, dk, dv), F32), pltpu.VMEM((heads, cb, cb), F32)],
        compiler_params=_cparams(1),
        name="ret_outproj",
    )(log_gamma, qkvg_first, qkvg_rest, x, w_out)


def _ple_kernel(h_ref, p_ref, pn_ref, wgate_ref, wproj_ref, nn_ref, h_out_ref, xn_out_ref, *,
                sub):
    for r in range(h_ref.shape[0] // sub):
        rows = slice(r * sub, (r + 1) * sub)
        h1 = h_ref[rows, :]
        half_emb = 0.5 * jnp.dot(p_ref[rows, :].astype(BF16), wproj_ref[...],
                                 preferred_element_type=F32)
        base = h1 + half_emb
        xn = _rms(h1, pn_ref[...]).astype(BF16)
        logits = jnp.dot(xn, wgate_ref[...], preferred_element_type=F32)
        h2 = base + half_emb * jnp.tanh(0.5 * logits)
        h_out_ref[rows, :] = h2
        xn_out_ref[rows, :] = _rms(h2, nn_ref[...]).astype(BF16)


def _ple(h, p, layer, ple_norm, w_gate, w_proj, next_norm, tm=512, sub=256):
    s, d = h.shape
    pd = p.shape[-1]
    row = lambda i: (i, 0)
    resident = lambda shape, idx: pl.BlockSpec(shape, lambda i: idx, pipeline_mode=pl.Buffered(1))
    return pl.pallas_call(
        functools.partial(_ple_kernel, sub=sub),
        grid=(s // tm,),
        in_specs=[pl.BlockSpec((tm, d), row),
                  pl.BlockSpec((None, None, tm, pd), lambda i: (layer, 0, i, 0)),
                  resident((None, 1, d), (layer, 0, 0)),
                  resident((None, d, d), (layer, 0, 0)),
                  resident((None, pd, d), (layer, 0, 0)),
                  resident((1, d), (0, 0))],
        out_specs=(pl.BlockSpec((tm, d), row), pl.BlockSpec((tm, d), row)),
        out_shape=(jax.ShapeDtypeStruct((s, d), F32), jax.ShapeDtypeStruct((s, d), BF16)),
        compiler_params=_cparams(1),
        name="ple",
    )(h, p, ple_norm, w_gate, w_proj, next_norm)


def _outproj_ple_final_kernel(z_ref, h_ref, p_ref, wout_hbm, pn_ref, wgate_hbm, wproj_hbm,
                              fn_ref, o_ref, wout_ref, wgate_ref, wproj_ref, sem, *,
                              layer, k_chunks):
    i = pl.program_id(0)
    kc = wout_ref.shape[0] // k_chunks

    def weight_copies():
        cps = [pltpu.make_async_copy(wout_hbm.at[pl.ds(k * kc, kc), :],
                                     wout_ref.at[pl.ds(k * kc, kc), :], sem.at[k])
               for k in range(k_chunks)]
        cps.append(pltpu.make_async_copy(wproj_hbm.at[layer], wproj_ref, sem.at[k_chunks]))
        cps.append(pltpu.make_async_copy(wgate_hbm.at[layer], wgate_ref, sem.at[k_chunks + 1]))
        return cps

    def body(first):
        cps = weight_copies() if first else None
        if first:
            h1 = h_ref[...]
            for k in range(k_chunks):
                cps[k].wait()
                h1 = h1 + jnp.dot(z_ref[:, k * kc:(k + 1) * kc], wout_ref[k * kc:(k + 1) * kc, :],
                                  preferred_element_type=F32)
            cps[k_chunks].wait()
        else:
            h1 = h_ref[...] + jnp.dot(z_ref[...], wout_ref[...], preferred_element_type=F32)
        half_emb = 0.5 * jnp.dot(p_ref[...].astype(BF16), wproj_ref[...],
                                 preferred_element_type=F32)
        base = h1 + half_emb
        xn = _rms(h1, pn_ref[...]).astype(BF16)
        if first:
            cps[k_chunks + 1].wait()
        logits = jnp.dot(xn, wgate_ref[...], preferred_element_type=F32)
        o_ref[...] = _rms(base + half_emb * jnp.tanh(0.5 * logits), fn_ref[...])

    @pl.when(i == 0)
    def _():
        for cp in weight_copies():
            cp.start()
        body(first=True)

    @pl.when(i > 0)
    def _():
        body(first=False)


def _outproj_ple_final(z, h, p, layer, w_out, ple_norm, w_gate, w_proj, final_norm, tm=256,
                       k_chunks=4):
    s, d = h.shape
    kz = z.shape[1]
    pd = p.shape[-1]
    row = lambda i: (i, 0)
    resident = lambda shape, idx: pl.BlockSpec(shape, lambda i: idx, pipeline_mode=pl.Buffered(1))
    hbm = pl.BlockSpec(memory_space=pl.ANY)
    return pl.pallas_call(
        functools.partial(_outproj_ple_final_kernel, layer=layer, k_chunks=k_chunks),
        grid=(s // tm,),
        in_specs=[pl.BlockSpec((tm, kz), row),
                  pl.BlockSpec((tm, d), row),
                  pl.BlockSpec((None, None, tm, pd), lambda i: (layer, 0, i, 0)),
                  hbm,
                  resident((None, 1, d), (layer, 0, 0)),
                  hbm,
                  hbm,
                  resident((1, d), (0, 0))],
        out_specs=pl.BlockSpec((None, tm, d), lambda i: (0, i, 0)),
        out_shape=jax.ShapeDtypeStruct((1, s, d), F32),
        scratch_shapes=[pltpu.VMEM((kz, d), BF16), pltpu.VMEM((d, d), BF16),
                        pltpu.VMEM((pd, d), BF16), pltpu.SemaphoreType.DMA((k_chunks + 2,))],
        compiler_params=_cparams(1),
        name="outproj_ple_final",
    )(z, h, p, w_out, ple_norm, w_gate, w_proj, final_norm)


def _pool_wfuse_kernel(wu_ref, wgrp_ref, o_ref):
    o_ref[...] = jnp.dot(wu_ref[...].astype(BF16), wgrp_ref[...].astype(BF16),
                         preferred_element_type=F32).astype(o_ref.dtype)


def _pool_wfuse(w_in, w_group):
    _, d, _ = w_in.shape
    _, n_groups, gd, _ = w_group.shape
    return pl.pallas_call(
        _pool_wfuse_kernel,
        grid=(n_groups,),
        in_specs=[pl.BlockSpec((None, d, gd), lambda g: (0, 0, g)),
                  pl.BlockSpec((None, None, gd, gd), lambda g: (0, g, 0, 0))],
        out_specs=pl.BlockSpec((d, gd), lambda g: (0, g)),
        out_shape=jax.ShapeDtypeStruct((d, n_groups * gd), BF16),
        compiler_params=_cparams(1),
        name="pool_wfuse",
    )(w_in, w_group)


def _pool_kernel(xn_ref, wf_ref, wg_ref, scale_ref, o_ref, xbuf_ref, s2_ref, s4_ref, s8_ref, *,
                 windows):
    i = pl.program_id(0)
    tm = xn_ref.shape[0]
    gd = o_ref.shape[1] // len(windows)
    hl = POOL_HALO
    end = hl + tm

    @pl.when(i == 0)
    def _():
        xbuf_ref[0:hl, :] = jnp.zeros((hl, xbuf_ref.shape[1]), F32)

    xn = xn_ref[...]
    x32 = xn.astype(F32)
    xbuf_ref[hl:end, :] = x32
    s2_ref[8:end, :] = xbuf_ref[8:end, :] + xbuf_ref[7:end - 1, :]
    s4_ref[16:end, :] = s2_ref[16:end, :] + s2_ref[14:end - 2, :]
    s8_ref[24:end, :] = s4_ref[24:end, :] + s4_ref[20:end - 4, :]
    wsums = {2: s2_ref[hl:end, :], 4: s4_ref[hl:end, :], 8: s8_ref[hl:end, :],
             16: s8_ref[hl:end, :] + s8_ref[hl - 8:end - 8, :]}
    xbuf_ref[0:hl, :] = xbuf_ref[tm:end, :]

    tok = i * tm + lax.broadcasted_iota(jnp.int32, (tm, 1), 0)
    for g, w in enumerate(windows):
        cols = slice(g * gd, (g + 1) * gd)
        gate = jnp.dot(xn, wg_ref[:, cols], preferred_element_type=F32)
        gate = gate * _sigmoid(gate)
        inv_cnt = 1.0 / jnp.minimum(tok + 1, w).astype(F32)
        dev = (wsums[w] * inv_cnt - x32).astype(BF16)
        mixed = jnp.dot(dev, wf_ref[:, cols], preferred_element_type=F32) * scale_ref[:, cols]
        o_ref[:, cols] = (mixed * gate).astype(o_ref.dtype)


def _pool_mixer(xn, w_fused, w_gate, scale, tm=256):
    s, d = xn.shape
    width = w_fused.shape[1]
    assert set(POOL_WINDOWS) == {2, 4, 8, 16} and POOL_HALO >= 2 * max(POOL_WINDOWS)
    halo_buf = pltpu.VMEM((POOL_HALO + tm, d), F32)
    resident = lambda shape: pl.BlockSpec(shape, lambda i: (0, 0), pipeline_mode=pl.Buffered(1))
    return pl.pallas_call(
        functools.partial(_pool_kernel, windows=POOL_WINDOWS),
        grid=(s // tm,),
        in_specs=[pl.BlockSpec((tm, d), lambda i: (i, 0)),
                  resident((d, width)),
                  resident((d, width)),
                  resident((1, width))],
        out_specs=pl.BlockSpec((tm, width), lambda i: (i, 0)),
        out_shape=jax.ShapeDtypeStruct((s, width), BF16),
        scratch_shapes=[halo_buf, halo_buf, halo_buf, halo_buf],
        compiler_params=_cparams(1),
        name="pool_mixer",
    )(xn, w_fused, w_gate, scale)


def _rotary_tables(seq, dk, tm):
    freq = ROPE_BASE ** (-jnp.linspace(0.0, 1.0, dk // 2, dtype=F32))
    ang_r = jnp.arange(tm, dtype=F32)[:, None] * freq[None, :]
    ang_t = (jnp.arange(seq // tm, dtype=F32) * tm)[:, None, None] * freq[None, None, :]
    return jnp.cos(ang_r), jnp.sin(ang_r), jnp.cos(ang_t), jnp.sin(ang_t)


def kernel(x, p, ret_norm, ret_w_in, ret_w_out, pool_norm, pool_w_in, pool_w_group, pool_scale,
           pool_w_out, ple_norm, ple_w_gate, ple_w_proj, final_norm):
    b, s, d = x.shape
    assert b == 1
    heads = RET_HEADS
    dk = d // heads
    dv = ret_w_out.shape[1] // heads
    assert ret_w_in.shape[2] == 2 * heads * dk + 2 * heads * dv
    tm_inproj = tn_inproj = 1024

    tabs = _rotary_tables(s, dk, tm_inproj)
    log_gamma = jnp.log1p(-(2.0 ** (-5.0 - jnp.arange(heads, dtype=F32))))
    ple_norm3 = ple_norm.reshape(ple_norm.shape[0], 1, d)
    n_layers, pd, _ = ple_w_proj.shape
    n_groups, gd = pool_w_group.shape[1:3]
    pool_width = n_groups * gd

    whole = (0, 1)
    side = [(ret_w_out.reshape(heads * dv, d), 64, whole),
            (pool_w_in.reshape(d, 2 * pool_width), 32, (1, 2)),
            (pool_w_out.reshape(pool_width, d), 64, whole),
            (ple_w_gate.reshape(n_layers * d, d), 64, whole),
            (ple_w_proj.reshape(n_layers * pd, d), 32, whole)]

    n_tiles = ret_w_in.shape[2] // tn_inproj
    widths = (heads, heads * dk, heads * dv, dk ** -0.5, tm_inproj)
    qkvg_first, (xn0,) = _ret_inproj(x, ret_w_in, tabs, log_gamma, [], *widths, 0, 1, gain=ret_norm,
                                     n_wchunks=1)
    qkvg_rest, (w_ret_out, w_pool_gate, w_pool_out, w_gate, w_proj) = _ret_inproj(
        xn0, ret_w_in, tabs, log_gamma, side, *widths, 1, n_tiles - 1)
    w_gate = w_gate.reshape(n_layers, d, d)
    w_proj = w_proj.reshape(n_layers, pd, d)
    h1 = _ret_outproj(qkvg_first, qkvg_rest, x, w_ret_out, log_gamma, heads, dk, dv)
    h2, xn2 = _ple(h1, p, 0, ple_norm3, w_gate, w_proj, pool_norm)

    w_pool_fused = _pool_wfuse(pool_w_in, pool_w_group)
    z = _pool_mixer(xn2, w_pool_fused, w_pool_gate, pool_scale)
    return _outproj_ple_final(z, h2, p, 1, w_pool_out, ple_norm3, w_gate, w_proj,
                              final_norm.reshape(1, d))
```

```python
import functools

import jax
import jax.numpy as jnp
from jax import lax
from jax.experimental import pallas as pl
from jax.experimental.pallas import tpu as pltpu

F32 = jnp.float32
BF16 = jnp.bfloat16

RET_HEADS = 8
ROPE_BASE = 10000.0
POOL_WINDOWS = (2, 4, 8, 16)
NORM_EPS = 1e-6
GN_EPS = 1e-5

LANES = 128
MXU_WIDTH = 256
POOL_HALO = 32
RET_BLOCK = 256
VMEM_LIMIT = 56 * 1024 * 1024


def _cparams(n_axes):
    return pltpu.CompilerParams(
        dimension_semantics=("arbitrary",) * n_axes,
        vmem_limit_bytes=VMEM_LIMIT,
    )


def _rms(x, gain):
    ms = jnp.mean(x * x, axis=-1, keepdims=True)
    return x * lax.rsqrt(ms + NORM_EPS) * gain


def _sigmoid(x):
    return 0.5 * jnp.tanh(0.5 * x) + 0.5


def _silu(x):
    h = 0.5 * x
    return h * jnp.tanh(h) + h


def _ret_inproj_kernel(lg_ref, lhs_ref, *rest, fuse_norm, n_wchunks, n_side, tile0, n_tiles,
                       heads, n_q_tiles, n_rot_tiles, n_plain_tiles, k_scale, cb):
    if fuse_norm:
        gain_ref, rest = rest[0], rest[1:]
    w_refs, rest = rest[:n_wchunks], rest[n_wchunks:]
    cr_ref, sr_ref, ct_ref, st_ref = rest[:4]
    rest = rest[4:]
    side_in, rest = rest[:n_side], rest[n_side:]
    o_ref, rest = rest[0], rest[1:]
    if fuse_norm:
        xn_ref, rest = rest[0], rest[1:]
    else:
        xn_ref = lhs_ref
    side_out, rest = rest[:n_side], rest[n_side:]
    wbf_ref, cs_ref, sn_ref = rest
    n = tile0 if n_tiles == 1 else pl.program_id(0) + tile0
    i = pl.program_id(1)
    tm, tn = o_ref.shape
    heads_per_tile = tn // MXU_WIDTH
    wrows = wbf_ref.shape[0] // n_wchunks

    def when(cond):
        if isinstance(cond, bool):
            return lambda f: f() if cond else None
        return pl.when(cond)

    @when(n >= n_rot_tiles)
    def _():
        @pl.when(i == 0)
        def _():
            for k, w_ref in enumerate(w_refs):
                wbf_ref[k * wrows:(k + 1) * wrows, :] = w_ref[...].astype(BF16)

    @when(n < n_rot_tiles)
    def _():
        @pl.when(i == 0)
        def _():
            src = lax.broadcasted_iota(jnp.int32, (MXU_WIDTH, MXU_WIDTH), 0)
            dst = lax.broadcasted_iota(jnp.int32, (MXU_WIDTH, MXU_WIDTH), 1)
            half = MXU_WIDTH // 2
            wanted = jnp.where(dst < half, 2 * dst, 2 * (dst - half) + 1)
            perm = (src == wanted).astype(BF16)
            for k, w_ref in enumerate(w_refs):
                for c in range(heads_per_tile):
                    cols = slice(c * MXU_WIDTH, (c + 1) * MXU_WIDTH)
                    wbf_ref[k * wrows:(k + 1) * wrows, cols] = jnp.dot(
                        w_ref[:, cols].astype(BF16), perm,
                        preferred_element_type=F32).astype(BF16)

    if fuse_norm:
        xn_ref[...] = _rms(lhs_ref[...], gain_ref[...]).astype(xn_ref.dtype)

    def run(epilogue):
        for src, dst in zip(side_in, side_out):
            dst[...] = src[...].astype(BF16)
        for c in range(heads_per_tile):
            cols = slice(c * MXU_WIDTH, (c + 1) * MXU_WIDTH)
            acc = jnp.dot(xn_ref[...], wbf_ref[:, cols], preferred_element_type=F32)
            o_ref[:, cols] = epilogue(acc, c).astype(o_ref.dtype)

    @when(n < n_rot_tiles)
    def _():
        ct, st = ct_ref[...], st_ref[...]
        cr, sr = cr_ref[...], sr_ref[...]
        cs_ref[...] = ct * cr - st * sr
        sn_ref[...] = st * cr + ct * sr
        is_k = n >= n_q_tiles
        row = lax.broadcasted_iota(jnp.int32, (tm, LANES), 0)
        j = (row & (cb - 1)).astype(F32)
        expo = jnp.where(is_k, cb - 1.0 - j, j + 1.0)
        scale = jnp.where(is_k, k_scale, 1.0).astype(F32)

        def rotate(acc, c):
            head = (n * heads_per_tile + c) % heads
            decay = jnp.exp(expo * lg_ref[head]) * scale
            cd = cs_ref[...] * decay
            sd = sn_ref[...] * decay
            xe, xo = acc[:, :LANES], acc[:, LANES:]
            return jnp.concatenate([xe * cd - xo * sd, xo * cd + xe * sd], axis=1)

        run(rotate)

    @when((n >= n_rot_tiles) & (n < n_plain_tiles))
    def _():
        run(lambda acc, c: acc)

    @when(n >= n_plain_tiles)
    def _():
        run(lambda acc, c: _silu(acc))


def _ret_inproj(lhs, w_in, tabs, log_gamma, side_weights, heads, qk_width, v_width, k_scale, tm,
                tile0, n_tiles, gain=None, tn=1024, cb=RET_BLOCK, n_wchunks=2):
    fuse_norm = gain is not None
    assert not (fuse_norm and (side_weights or n_tiles != 1))
    s, d = lhs.shape[-2:]
    cr, sr, ct, st = tabs
    head_w = cr.shape[1]
    assert 2 * head_w == MXU_WIDTH == 2 * LANES and cr.shape[0] == tm
    assert tm % cb == 0 and cb & (cb - 1) == 0
    n_i = s // tm
    n_rot_tiles = 2 * qk_width // tn
    kern = functools.partial(_ret_inproj_kernel, fuse_norm=fuse_norm, n_wchunks=n_wchunks,
                             n_side=len(side_weights),
                             tile0=tile0, n_tiles=n_tiles, heads=heads, n_q_tiles=qk_width // tn,
                             n_rot_tiles=n_rot_tiles, n_plain_tiles=n_rot_tiles + v_width // tn,
                             k_scale=k_scale, cb=cb)
    side_in_specs, side_specs, side_shapes = [], [], []
    for arr, chunk_rows, (col_blk, n_col_blks) in side_weights:
        rows, cols = arr.shape[0], arr.shape[1] // n_col_blks
        n_chunks = rows // chunk_rows
        assert n_chunks * chunk_rows == rows and n_chunks <= n_tiles * n_i
        in_map = lambda n, i, last=n_chunks - 1, cb_=col_blk: (jnp.minimum(n * n_i + i, last), cb_)
        out_map = lambda n, i, last=n_chunks - 1: (jnp.minimum(n * n_i + i, last), 0)
        side_in_specs.append(pl.BlockSpec((chunk_rows, cols), in_map))
        side_specs.append(pl.BlockSpec((chunk_rows, cols), out_map))
        side_shapes.append(jax.ShapeDtypeStruct((rows, cols), BF16))
    if fuse_norm:
        lhs_specs = [pl.BlockSpec((None, tm, d), lambda n, i: (0, i, 0)),
                     pl.BlockSpec((1, d), lambda n, i: (0, 0))]
        lhs_args = (lhs, gain)
        extra_specs = [pl.BlockSpec((tm, d), lambda n, i: (i, 0))]
        extra_shapes = [jax.ShapeDtypeStruct((s, d), BF16)]
        w_mode = dict(pipeline_mode=pl.Buffered(1))
    else:
        lhs_specs = [pl.BlockSpec((tm, d), lambda n, i: (i, 0))]
        lhs_args = (lhs,)
        extra_specs, extra_shapes, w_mode = [], [], {}
    last_tile = tile0 + n_tiles - 1
    stride = n_i // n_wchunks
    assert stride * n_wchunks == n_i and d % n_wchunks == 0

    def w_map(n, i, k):
        ahead = jnp.where(i > k * stride, 1, 0)
        return (0, k, jnp.minimum(n + tile0 + ahead, last_tile))

    w_specs = [pl.BlockSpec((None, d // n_wchunks, tn), functools.partial(w_map, k=k), **w_mode)
               for k in range(n_wchunks)]
    const = lambda n, i: (0, 0)
    outs = pl.pallas_call(
        kern,
        grid=(n_tiles, n_i),
        in_specs=[pl.BlockSpec(memory_space=pltpu.SMEM)] + lhs_specs + w_specs + [
            pl.BlockSpec((tm, head_w), const),
            pl.BlockSpec((tm, head_w), const),
            pl.BlockSpec((None, 1, head_w), lambda n, i: (i, 0, 0)),
            pl.BlockSpec((None, 1, head_w), lambda n, i: (i, 0, 0))] + side_in_specs,
        out_specs=[pl.BlockSpec((tm, tn), lambda n, i: (i, n))] + extra_specs + side_specs,
        out_shape=[jax.ShapeDtypeStruct((s, n_tiles * tn), BF16)] + extra_shapes + side_shapes,
        scratch_shapes=[pltpu.VMEM((d, tn), BF16),
                        pltpu.VMEM((tm, head_w), F32),
                        pltpu.VMEM((tm, head_w), F32)],
        compiler_params=_cparams(2),
        name="ret_inproj_first" if fuse_norm else "ret_inproj",
    )(log_gamma, *lhs_args, *([w_in] * n_wchunks), cr, sr, ct, st,
      *[sw[0] for sw in side_weights])
    return outs[0], outs[1:]


def _ret_outproj_kernel(lg_ref, first_ref, rest_ref, x_ref, wout_hbm, o_ref, state_ref, dmat_ref,
                        wout_ref, sem, *, heads, dk, dv):
    t = pl.program_id(0)
    cb, n_first = first_ref.shape

    def weight_copies():
        return [pltpu.make_async_copy(wout_hbm.at[pl.ds(h * dv, dv), :],
                                      wout_ref.at[pl.ds(h * dv, dv), :], sem.at[h])
                for h in range(heads)]

    def cols(start, width):
        if start + width <= n_first:
            return first_ref[:, start:start + width]
        assert start >= n_first
        return rest_ref[:, start - n_first:start - n_first + width]

    @pl.when(t == 0)
    def _():
        for cp in weight_copies():
            cp.start()
        state_ref[...] = jnp.zeros_like(state_ref)
        r = lax.broadcasted_iota(jnp.int32, (cb, cb), 0)
        c = lax.broadcasted_iota(jnp.int32, (cb, cb), 1)
        for h in range(heads):
            undo = jnp.exp(jnp.full((cb, cb), -cb, F32) * lg_ref[h])
            dmat_ref[h] = jnp.where(r >= c, undo, 0.0)

    k0, v0, g0 = heads * dk, 2 * heads * dk, 2 * heads * dk + heads * dv

    def mix(h):
        qd = cols(h * dk, dk)
        kd = cols(k0 + h * dk, dk)
        v = cols(v0 + h * dv, dv)
        block_decay = jnp.exp(jnp.full((1, 1), cb, F32) * lg_ref[h])
        scores = lax.dot_general(qd, kd, (((1,), (1,)), ((), ())), preferred_element_type=F32)
        state = state_ref[h]
        cross = jnp.dot(qd, state.astype(BF16), preferred_element_type=F32)
        update = lax.dot_general(kd, v, (((0,), (0,)), ((), ())), preferred_element_type=F32)
        p = (scores * dmat_ref[h]).astype(BF16)
        y = cross + jnp.dot(p, v, preferred_element_type=F32)
        state_ref[h] = state * block_decay + update
        return y

    def norm_gate(h, y):
        gate = cols(g0 + h * dv, dv)
        mu = jnp.mean(y, axis=-1, keepdims=True)
        yc = y - mu
        var = jnp.mean(yc * yc, axis=-1, keepdims=True)
        return (yc * lax.rsqrt(var + GN_EPS)).astype(BF16) * gate

    def project(h, yg, acc):
        return acc + jnp.dot(yg, wout_ref[h * dv:(h + 1) * dv, :], preferred_element_type=F32)

    def body(first):
        cps = weight_copies() if first else None
        acc = x_ref[...]
        ys, ygs = {}, {}
        for step in range(heads + 2):
            if 0 <= step - 2 < heads:
                if first:
                    cps[step - 2].wait()
                acc = project(step - 2, ygs.pop(step - 2), acc)
            if step < heads:
                ys[step] = mix(step)
            if 0 <= step - 1 < heads:
                ygs[step - 1] = norm_gate(step - 1, ys.pop(step - 1))
        o_ref[...] = acc

    @pl.when(t == 0)
    def _():
        body(first=True)

    @pl.when(t > 0)
    def _():
        body(first=False)


def _ret_outproj(qkvg_first, qkvg_rest, x, w_out, log_gamma, heads, dk, dv, cb=RET_BLOCK):
    s, n_first = qkvg_first.shape
    n_rest = qkvg_rest.shape[1]
    d = w_out.shape[1]
    return pl.pallas_call(
        functools.partial(_ret_outproj_kernel, heads=heads, dk=dk, dv=dv),
        grid=(s // cb,),
        in_specs=[pl.BlockSpec(memory_space=pltpu.SMEM),
                  pl.BlockSpec((cb, n_first), lambda t: (t, 0)),
                  pl.BlockSpec((cb, n_rest), lambda t: (t, 0)),
                  pl.BlockSpec((None, cb, d), lambda t: (0, t, 0)),
                  pl.BlockSpec(memory_space=pl.ANY)],
        out_specs=pl.BlockSpec((cb, d), lambda t: (t, 0)),
        out_shape=jax.ShapeDtypeStruct((s, d), F32),
        scratch_shapes=[pltpu.VMEM((heads, dk, dv), F32), pltpu.VMEM((heads, cb, cb), F32),
                        pltpu.VMEM((heads * dv, d), BF16), pltpu.SemaphoreType.DMA((heads,))],
        compiler_params=_cparams(1),
        name="ret_outproj",
    )(log_gamma, qkvg_first, qkvg_rest, x, w_out)


def _ple_kernel(h_ref, p_ref, pn_ref, wgate_ref, wproj_ref, nn_ref, h_out_ref, xn_out_ref, *,
                sub):
    for r in range(h_ref.shape[0] // sub):
        rows = slice(r * sub, (r + 1) * sub)
        h1 = h_ref[rows, :]
        half_emb = 0.5 * jnp.dot(p_ref[rows, :].astype(BF16), wproj_ref[...],
                                 preferred_element_type=F32)
        base = h1 + half_emb
        xn = _rms(h1, pn_ref[...]).astype(BF16)
        logits = jnp.dot(xn, wgate_ref[...], preferred_element_type=F32)
        h2 = base + half_emb * jnp.tanh(0.5 * logits)
        h_out_ref[rows, :] = h2
        xn_out_ref[rows, :] = _rms(h2, nn_ref[...]).astype(BF16)


def _ple(h, p, layer, ple_norm, w_gate, w_proj, next_norm, tm=512, sub=256):
    s, d = h.shape
    pd = p.shape[-1]
    row = lambda i: (i, 0)
    resident = lambda shape, idx: pl.BlockSpec(shape, lambda i: idx, pipeline_mode=pl.Buffered(1))
    return pl.pallas_call(
        functools.partial(_ple_kernel, sub=sub),
        grid=(s // tm,),
        in_specs=[pl.BlockSpec((tm, d), row),
                  pl.BlockSpec((None, None, tm, pd), lambda i: (layer, 0, i, 0)),
                  resident((None, 1, d), (layer, 0, 0)),
                  resident((None, d, d), (layer, 0, 0)),
                  resident((None, pd, d), (layer, 0, 0)),
                  resident((1, d), (0, 0))],
        out_specs=(pl.BlockSpec((tm, d), row), pl.BlockSpec((tm, d), row)),
        out_shape=(jax.ShapeDtypeStruct((s, d), F32), jax.ShapeDtypeStruct((s, d), BF16)),
        compiler_params=_cparams(1),
        name="ple",
    )(h, p, ple_norm, w_gate, w_proj, next_norm)


def _outproj_ple_final_kernel(z_ref, h_ref, p_ref, wout_hbm, pn_ref, wgate_hbm, wproj_hbm,
                              fn_ref, o_ref, wout_ref, wgate_ref, wproj_ref, sem, *,
                              layer, k_chunks):
    i = pl.program_id(0)
    kc = wout_ref.shape[0] // k_chunks

    def weight_copies():
        cps = [pltpu.make_async_copy(wout_hbm.at[pl.ds(k * kc, kc), :],
                                     wout_ref.at[pl.ds(k * kc, kc), :], sem.at[k])
               for k in range(k_chunks)]
        cps.append(pltpu.make_async_copy(wproj_hbm.at[layer], wproj_ref, sem.at[k_chunks]))
        cps.append(pltpu.make_async_copy(wgate_hbm.at[layer], wgate_ref, sem.at[k_chunks + 1]))
        return cps

    def body(first):
        cps = weight_copies() if first else None
        if first:
            h1 = h_ref[...]
            for k in range(k_chunks):
                cps[k].wait()
                h1 = h1 + jnp.dot(z_ref[:, k * kc:(k + 1) * kc], wout_ref[k * kc:(k + 1) * kc, :],
                                  preferred_element_type=F32)
            cps[k_chunks].wait()
        else:
            h1 = h_ref[...] + jnp.dot(z_ref[...], wout_ref[...], preferred_element_type=F32)
        half_emb = 0.5 * jnp.dot(p_ref[...].astype(BF16), wproj_ref[...],
                                 preferred_element_type=F32)
        base = h1 + half_emb
        xn = _rms(h1, pn_ref[...]).astype(BF16)
        if first:
            cps[k_chunks + 1].wait()
        logits = jnp.dot(xn, wgate_ref[...], preferred_element_type=F32)
        o_ref[...] = _rms(base + half_emb * jnp.tanh(0.5 * logits), fn_ref[...])

    @pl.when(i == 0)
    def _():
        for cp in weight_copies():
            cp.start()
        body(first=True)

    @pl.when(i > 0)
    def _():
        body(first=False)


def _outproj_ple_final(z, h, p, layer, w_out, ple_norm, w_gate, w_proj, final_norm, tm=256,
                       k_chunks=4):
    s, d = h.shape
    kz = z.shape[1]
    pd = p.shape[-1]
    row = lambda i: (i, 0)
    resident = lambda shape, idx: pl.BlockSpec(shape, lambda i: idx, pipeline_mode=pl.Buffered(1))
    hbm = pl.BlockSpec(memory_space=pl.ANY)
    return pl.pallas_call(
        functools.partial(_outproj_ple_final_kernel, layer=layer, k_chunks=k_chunks),
        grid=(s // tm,),
        in_specs=[pl.BlockSpec((tm, kz), row),
                  pl.BlockSpec((tm, d), row),
                  pl.BlockSpec((None, None, tm, pd), lambda i: (layer, 0, i, 0)),
                  hbm,
                  resident((None, 1, d), (layer, 0, 0)),
                  hbm,
                  hbm,
                  resident((1, d), (0, 0))],
        out_specs=pl.BlockSpec((None, tm, d), lambda i: (0, i, 0)),
        out_shape=jax.ShapeDtypeStruct((1, s, d), F32),
        scratch_shapes=[pltpu.VMEM((kz, d), BF16), pltpu.VMEM((d, d), BF16),
                        pltpu.VMEM((pd, d), BF16), pltpu.SemaphoreType.DMA((k_chunks + 2,))],
        compiler_params=_cparams(1),
        name="outproj_ple_final",
    )(z, h, p, w_out, ple_norm, w_gate, w_proj, final_norm)


def _pool_wfuse_kernel(wu_ref, wgrp_ref, o_ref):
    o_ref[...] = jnp.dot(wu_ref[...].astype(BF16), wgrp_ref[...].astype(BF16),
                         preferred_element_type=F32).astype(o_ref.dtype)


def _pool_wfuse(w_in, w_group):
    _, d, _ = w_in.shape
    _, n_groups, gd, _ = w_group.shape
    return pl.pallas_call(
        _pool_wfuse_kernel,
        grid=(n_groups,),
        in_specs=[pl.BlockSpec((None, d, gd), lambda g: (0, 0, g)),
                  pl.BlockSpec((None, None, gd, gd), lambda g: (0, g, 0, 0))],
        out_specs=pl.BlockSpec((d, gd), lambda g: (0, g)),
        out_shape=jax.ShapeDtypeStruct((d, n_groups * gd), BF16),
        compiler_params=_cparams(1),
        name="pool_wfuse",
    )(w_in, w_group)


def _pool_kernel(xn_ref, wf_ref, wg_ref, scale_ref, o_ref, xbuf_ref, s2_ref, s4_ref, s8_ref, *,
                 windows):
    i = pl.program_id(0)
    tm = xn_ref.shape[0]
    gd = o_ref.shape[1] // len(windows)
    hl = POOL_HALO
    end = hl + tm

    @pl.when(i == 0)
    def _():
        xbuf_ref[0:hl, :] = jnp.zeros((hl, xbuf_ref.shape[1]), F32)

    xn = xn_ref[...]
    x32 = xn.astype(F32)
    xbuf_ref[hl:end, :] = x32
    s2_ref[8:end, :] = xbuf_ref[8:end, :] + xbuf_ref[7:end - 1, :]
    s4_ref[16:end, :] = s2_ref[16:end, :] + s2_ref[14:end - 2, :]
    s8_ref[24:end, :] = s4_ref[24:end, :] + s4_ref[20:end - 4, :]
    wsums = {2: s2_ref[hl:end, :], 4: s4_ref[hl:end, :], 8: s8_ref[hl:end, :],
             16: s8_ref[hl:end, :] + s8_ref[hl - 8:end - 8, :]}
    xbuf_ref[0:hl, :] = xbuf_ref[tm:end, :]

    tok = i * tm + lax.broadcasted_iota(jnp.int32, (tm, 1), 0)
    for g, w in enumerate(windows):
        cols = slice(g * gd, (g + 1) * gd)
        gate = jnp.dot(xn, wg_ref[:, cols], preferred_element_type=F32)
        gate = gate * _sigmoid(gate)
        inv_cnt = 1.0 / jnp.minimum(tok + 1, w).astype(F32)
        dev = (wsums[w] * inv_cnt - x32).astype(BF16)
        mixed = jnp.dot(dev, wf_ref[:, cols], preferred_element_type=F32) * scale_ref[:, cols]
        o_ref[:, cols] = (mixed * gate).astype(o_ref.dtype)


def _pool_mixer(xn, w_fused, w_gate, scale, tm=256):
    s, d = xn.shape
    width = w_fused.shape[1]
    assert set(POOL_WINDOWS) == {2, 4, 8, 16} and POOL_HALO >= 2 * max(POOL_WINDOWS)
    halo_buf = pltpu.VMEM((POOL_HALO + tm, d), F32)
    resident = lambda shape: pl.BlockSpec(shape, lambda i: (0, 0), pipeline_mode=pl.Buffered(1))
    return pl.pallas_call(
        functools.partial(_pool_kernel, windows=POOL_WINDOWS),
        grid=(s // tm,),
        in_specs=[pl.BlockSpec((tm, d), lambda i: (i, 0)),
                  resident((d, width)),
                  resident((d, width)),
                  resident((1, width))],
        out_specs=pl.BlockSpec((tm, width), lambda i: (i, 0)),
        out_shape=jax.ShapeDtypeStruct((s, width), BF16),
        scratch_shapes=[halo_buf, halo_buf, halo_buf, halo_buf],
        compiler_params=_cparams(1),
        name="pool_mixer",
    )(xn, w_fused, w_gate, scale)


def _rotary_tables(seq, dk, tm):
    freq = ROPE_BASE ** (-jnp.linspace(0.0, 1.0, dk // 2, dtype=F32))
    ang_r = jnp.arange(tm, dtype=F32)[:, None] * freq[None, :]
    ang_t = (jnp.arange(seq // tm, dtype=F32) * tm)[:, None, None] * freq[None, None, :]
    return jnp.cos(ang_r), jnp.sin(ang_r), jnp.cos(ang_t), jnp.sin(ang_t)


def kernel(x, p, ret_norm, ret_w_in, ret_w_out, pool_norm, pool_w_in, pool_w_group, pool_scale,
           pool_w_out, ple_norm, ple_w_gate, ple_w_proj, final_norm):
    b, s, d = x.shape
    assert b == 1
    heads = RET_HEADS
    dk = d // heads
    dv = ret_w_out.shape[1] // heads
    assert ret_w_in.shape[2] == 2 * heads * dk + 2 * heads * dv
    tm_inproj = tn_inproj = 1024

    tabs = _rotary_tables(s, dk, tm_inproj)
    log_gamma = jnp.log1p(-(2.0 ** (-5.0 - jnp.arange(heads, dtype=F32))))
    ple_norm3 = ple_norm.reshape(ple_norm.shape[0], 1, d)
    n_layers, pd, _ = ple_w_proj.shape
    n_groups, gd = pool_w_group.shape[1:3]
    pool_width = n_groups * gd

    whole = (0, 1)
    side = [(ret_w_out.reshape(heads * dv, d), 64, whole),
            (pool_w_in.reshape(d, 2 * pool_width), 32, (1, 2)),
            (pool_w_out.reshape(pool_width, d), 64, whole),
            (ple_w_gate.reshape(n_layers * d, d), 64, whole),
            (ple_w_proj.reshape(n_layers * pd, d), 32, whole)]

    n_tiles = ret_w_in.shape[2] // tn_inproj
    widths = (heads, heads * dk, heads * dv, dk ** -0.5, tm_inproj)
    qkvg_first, (xn0,) = _ret_inproj(x, ret_w_in, tabs, log_gamma, [], *widths, 0, 1, gain=ret_norm,
                                     n_wchunks=1)
    qkvg_rest, (w_ret_out, w_pool_gate, w_pool_out, w_gate, w_proj) = _ret_inproj(
        xn0, ret_w_in, tabs, log_gamma, side, *widths, 1, n_tiles - 1)
    w_gate = w_gate.reshape(n_layers, d, d)
    w_proj = w_proj.reshape(n_layers, pd, d)
    h1 = _ret_outproj(qkvg_first, qkvg_rest, x, w_ret_out, log_gamma, heads, dk, dv)
    h2, xn2 = _ple(h1, p, 0, ple_norm3, w_gate, w_proj, pool_norm)

    w_pool_fused = _pool_wfuse(pool_w_in, pool_w_group)
    z = _pool_mixer(xn2, w_pool_fused, w_pool_gate, pool_scale)
    return _outproj_ple_final(z, h2, p, 1, w_pool_out, ple_norm3, w_gate, w_proj,
                              final_norm.reshape(1, d))
```

```python
import functools

import jax
import jax.numpy as jnp
from jax import lax
from jax.experimental import pallas as pl
from jax.experimental.pallas import tpu as pltpu

F32 = jnp.float32
BF16 = jnp.bfloat16

RET_HEADS = 8
ROPE_BASE = 10000.0
POOL_WINDOWS = (2, 4, 8, 16)
NORM_EPS = 1e-6
GN_EPS = 1e-5

LANES = 128
MXU_WIDTH = 256
POOL_HALO = 32
RET_BLOCK = 256
VMEM_LIMIT = 56 * 1024 * 1024
POOL_VMEM_LIMIT = 60 * 1024 * 1024


def _cparams(n_axes, vmem_limit=VMEM_LIMIT):
    return pltpu.CompilerParams(
        dimension_semantics=("arbitrary",) * n_axes,
        vmem_limit_bytes=vmem_limit,
    )


def _rms(x, gain):
    ms = jnp.mean(x * x, axis=-1, keepdims=True)
    return x * lax.rsqrt(ms + NORM_EPS) * gain


def _sigmoid(x):
    return 0.5 * jnp.tanh(0.5 * x) + 0.5


def _silu(x):
    h = 0.5 * x
    return h * jnp.tanh(h) + h


def _ret_inproj_kernel(lg_ref, lhs_ref, *rest, fuse_norm, n_wchunks, n_side, tile0, n_tiles,
                       heads, n_q_tiles, n_rot_tiles, n_plain_tiles, k_scale, cb):
    if fuse_norm:
        gain_ref, rest = rest[0], rest[1:]
    w_refs, rest = rest[:n_wchunks], rest[n_wchunks:]
    cr_ref, sr_ref, ct_ref, st_ref = rest[:4]
    rest = rest[4:]
    side_in, rest = rest[:n_side], rest[n_side:]
    o_ref, rest = rest[0], rest[1:]
    if fuse_norm:
        xn_ref, rest = rest[0], rest[1:]
    else:
        xn_ref = lhs_ref
    side_out, rest = rest[:n_side], rest[n_side:]
    wbf_ref, cs_ref, sn_ref = rest
    n = tile0 if n_tiles == 1 else pl.program_id(0) + tile0
    i = pl.program_id(1)
    tm, tn = o_ref.shape
    heads_per_tile = tn // MXU_WIDTH
    wrows = wbf_ref.shape[0] // n_wchunks

    def when(cond):
        if isinstance(cond, bool):
            return lambda f: f() if cond else None
        return pl.when(cond)

    @when(n >= n_rot_tiles)
    def _():
        @pl.when(i == 0)
        def _():
            for k, w_ref in enumerate(w_refs):
                wbf_ref[k * wrows:(k + 1) * wrows, :] = w_ref[...].astype(BF16)

    @when(n < n_rot_tiles)
    def _():
        @pl.when(i == 0)
        def _():
            src = lax.broadcasted_iota(jnp.int32, (MXU_WIDTH, MXU_WIDTH), 0)
            dst = lax.broadcasted_iota(jnp.int32, (MXU_WIDTH, MXU_WIDTH), 1)
            half = MXU_WIDTH // 2
            wanted = jnp.where(dst < half, 2 * dst, 2 * (dst - half) + 1)
            perm = (src == wanted).astype(BF16)
            for k, w_ref in enumerate(w_refs):
                for c in range(heads_per_tile):
                    cols = slice(c * MXU_WIDTH, (c + 1) * MXU_WIDTH)
                    wbf_ref[k * wrows:(k + 1) * wrows, cols] = jnp.dot(
                        w_ref[:, cols].astype(BF16), perm,
                        preferred_element_type=F32).astype(BF16)

    if fuse_norm:
        xn_ref[...] = _rms(lhs_ref[...], gain_ref[...]).astype(xn_ref.dtype)

    def run(epilogue):
        for src, dst in zip(side_in, side_out):
            dst[...] = src[...].astype(BF16)
        for c in range(heads_per_tile):
            cols = slice(c * MXU_WIDTH, (c + 1) * MXU_WIDTH)
            acc = jnp.dot(xn_ref[...], wbf_ref[:, cols], preferred_element_type=F32)
            o_ref[:, cols] = epilogue(acc, c).astype(o_ref.dtype)

    @when(n < n_rot_tiles)
    def _():
        ct, st = ct_ref[...], st_ref[...]
        cr, sr = cr_ref[...], sr_ref[...]
        cs_ref[...] = ct * cr - st * sr
        sn_ref[...] = st * cr + ct * sr
        is_k = n >= n_q_tiles
        row = lax.broadcasted_iota(jnp.int32, (tm, LANES), 0)
        j = (row & (cb - 1)).astype(F32)
        expo = jnp.where(is_k, cb - 1.0 - j, j + 1.0)
        scale = jnp.where(is_k, k_scale, 1.0).astype(F32)

        def rotate(acc, c):
            head = (n * heads_per_tile + c) % heads
            decay = jnp.exp(expo * lg_ref[head]) * scale
            cd = cs_ref[...] * decay
            sd = sn_ref[...] * decay
            xe, xo = acc[:, :LANES], acc[:, LANES:]
            return jnp.concatenate([xe * cd - xo * sd, xo * cd + xe * sd], axis=1)

        run(rotate)

    @when((n >= n_rot_tiles) & (n < n_plain_tiles))
    def _():
        run(lambda acc, c: acc)

    @when(n >= n_plain_tiles)
    def _():
        run(lambda acc, c: _silu(acc))


def _ret_inproj(lhs, w_in, tabs, log_gamma, side_weights, heads, qk_width, v_width, k_scale, tm,
                tile0, n_tiles, gain=None, tn=1024, cb=RET_BLOCK, n_wchunks=2):
    fuse_norm = gain is not None
    assert not (fuse_norm and (side_weights or n_tiles != 1))
    s, d = lhs.shape[-2:]
    cr, sr, ct, st = tabs
    head_w = cr.shape[1]
    assert 2 * head_w == MXU_WIDTH == 2 * LANES and cr.shape[0] == tm
    assert tm % cb == 0 and cb & (cb - 1) == 0
    n_i = s // tm
    n_rot_tiles = 2 * qk_width // tn
    kern = functools.partial(_ret_inproj_kernel, fuse_norm=fuse_norm, n_wchunks=n_wchunks,
                             n_side=len(side_weights),
                             tile0=tile0, n_tiles=n_tiles, heads=heads, n_q_tiles=qk_width // tn,
                             n_rot_tiles=n_rot_tiles, n_plain_tiles=n_rot_tiles + v_width // tn,
                             k_scale=k_scale, cb=cb)
    side_in_specs, side_specs, side_shapes = [], [], []
    for arr, chunk_rows, (col_blk, n_col_blks) in side_weights:
        rows, cols = arr.shape[0], arr.shape[1] // n_col_blks
        n_chunks = rows // chunk_rows
        assert n_chunks * chunk_rows == rows and n_chunks <= n_tiles * n_i
        in_map = lambda n, i, last=n_chunks - 1, cb_=col_blk: (jnp.minimum(n * n_i + i, last), cb_)
        out_map = lambda n, i, last=n_chunks - 1: (jnp.minimum(n * n_i + i, last), 0)
        side_in_specs.append(pl.BlockSpec((chunk_rows, cols), in_map))
        side_specs.append(pl.BlockSpec((chunk_rows, cols), out_map))
        side_shapes.append(jax.ShapeDtypeStruct((rows, cols), BF16))
    if fuse_norm:
        lhs_specs = [pl.BlockSpec((None, tm, d), lambda n, i: (0, i, 0)),
                     pl.BlockSpec((1, d), lambda n, i: (0, 0))]
        lhs_args = (lhs, gain)
        extra_specs = [pl.BlockSpec((tm, d), lambda n, i: (i, 0))]
        extra_shapes = [jax.ShapeDtypeStruct((s, d), BF16)]
        w_mode = dict(pipeline_mode=pl.Buffered(1))
    else:
        lhs_specs = [pl.BlockSpec((tm, d), lambda n, i: (i, 0))]
        lhs_args = (lhs,)
        extra_specs, extra_shapes, w_mode = [], [], {}
    last_tile = tile0 + n_tiles - 1
    stride = n_i // n_wchunks
    assert stride * n_wchunks == n_i and d % n_wchunks == 0

    def w_map(n, i, k):
        ahead = jnp.where(i > k * stride, 1, 0)
        return (0, k, jnp.minimum(n + tile0 + ahead, last_tile))

    w_specs = [pl.BlockSpec((None, d // n_wchunks, tn), functools.partial(w_map, k=k), **w_mode)
               for k in range(n_wchunks)]
    const = lambda n, i: (0, 0)
    outs = pl.pallas_call(
        kern,
        grid=(n_tiles, n_i),
        in_specs=[pl.BlockSpec(memory_space=pltpu.SMEM)] + lhs_specs + w_specs + [
            pl.BlockSpec((tm, head_w), const),
            pl.BlockSpec((tm, head_w), const),
            pl.BlockSpec((None, 1, head_w), lambda n, i: (i, 0, 0)),
            pl.BlockSpec((None, 1, head_w), lambda n, i: (i, 0, 0))] + side_in_specs,
        out_specs=[pl.BlockSpec((tm, tn), lambda n, i: (i, n))] + extra_specs + side_specs,
        out_shape=[jax.ShapeDtypeStruct((s, n_tiles * tn), BF16)] + extra_shapes + side_shapes,
        scratch_shapes=[pltpu.VMEM((d, tn), BF16),
                        pltpu.VMEM((tm, head_w), F32),
                        pltpu.VMEM((tm, head_w), F32)],
        compiler_params=_cparams(2),
        name="ret_inproj_first" if fuse_norm else "ret_inproj",
    )(log_gamma, *lhs_args, *([w_in] * n_wchunks), cr, sr, ct, st,
      *[sw[0] for sw in side_weights])
    return outs[0], outs[1:]


def _ret_outproj_kernel(lg_ref, first_ref, rest_ref, x_ref, wout_hbm, o_ref, state_ref, dmat_ref,
                        wout_ref, sem, *, heads, dk, dv):
    t = pl.program_id(0)
    cb, n_first = first_ref.shape

    def weight_copies():
        return [pltpu.make_async_copy(wout_hbm.at[pl.ds(h * dv, dv), :],
                                      wout_ref.at[pl.ds(h * dv, dv), :], sem.at[h])
                for h in range(heads)]

    def cols(start, width):
        if start + width <= n_first:
            return first_ref[:, start:start + width]
        assert start >= n_first
        return rest_ref[:, start - n_first:start - n_first + width]

    @pl.when(t == 0)
    def _():
        for cp in weight_copies():
            cp.start()
        state_ref[...] = jnp.zeros_like(state_ref)
        r = lax.broadcasted_iota(jnp.int32, (cb, cb), 0)
        c = lax.broadcasted_iota(jnp.int32, (cb, cb), 1)
        for h in range(heads):
            undo = jnp.exp(jnp.full((cb, cb), -cb, F32) * lg_ref[h])
            dmat_ref[h] = jnp.where(r >= c, undo, 0.0)

    k0, v0, g0 = heads * dk, 2 * heads * dk, 2 * heads * dk + heads * dv

    def mix(h):
        qd = cols(h * dk, dk)
        kd = cols(k0 + h * dk, dk)
        v = cols(v0 + h * dv, dv)
        block_decay = jnp.exp(jnp.full((1, 1), cb, F32) * lg_ref[h])
        scores = lax.dot_general(qd, kd, (((1,), (1,)), ((), ())), preferred_element_type=F32)
        state = state_ref[h]
        cross = jnp.dot(qd, state.astype(BF16), preferred_element_type=F32)
        update = lax.dot_general(kd, v, (((0,), (0,)), ((), ())), preferred_element_type=F32)
        p = (scores * dmat_ref[h]).astype(BF16)
        y = cross + jnp.dot(p, v, preferred_element_type=F32)
        state_ref[h] = state * block_decay + update
        return y

    def norm_gate(h, y):
        gate = cols(g0 + h * dv, dv)
        mu = jnp.mean(y, axis=-1, keepdims=True)
        yc = y - mu
        var = jnp.mean(yc * yc, axis=-1, keepdims=True)
        return (yc * lax.rsqrt(var + GN_EPS)).astype(BF16) * gate

    def project(h, yg, acc):
        return acc + jnp.dot(yg, wout_ref[h * dv:(h + 1) * dv, :], preferred_element_type=F32)

    def body(first):
        cps = weight_copies() if first else None
        acc = x_ref[...]
        ys, ygs = {}, {}
        for step in range(heads + 2):
            if 0 <= step - 2 < heads:
                if first:
                    cps[step - 2].wait()
                acc = project(step - 2, ygs.pop(step - 2), acc)
            if step < heads:
                ys[step] = mix(step)
            if 0 <= step - 1 < heads:
                ygs[step - 1] = norm_gate(step - 1, ys.pop(step - 1))
        o_ref[...] = acc

    @pl.when(t == 0)
    def _():
        body(first=True)

    @pl.when(t > 0)
    def _():
        body(first=False)


def _ret_outproj(qkvg_first, qkvg_rest, x, w_out, log_gamma, heads, dk, dv, cb=RET_BLOCK):
    s, n_first = qkvg_first.shape
    n_rest = qkvg_rest.shape[1]
    d = w_out.shape[1]
    return pl.pallas_call(
        functools.partial(_ret_outproj_kernel, heads=heads, dk=dk, dv=dv),
        grid=(s // cb,),
        in_specs=[pl.BlockSpec(memory_space=pltpu.SMEM),
                  pl.BlockSpec((cb, n_first), lambda t: (t, 0)),
                  pl.BlockSpec((cb, n_rest), lambda t: (t, 0)),
                  pl.BlockSpec((None, cb, d), lambda t: (0, t, 0)),
                  pl.BlockSpec(memory_space=pl.ANY)],
        out_specs=pl.BlockSpec((cb, d), lambda t: (t, 0)),
        out_shape=jax.ShapeDtypeStruct((s, d), F32),
        scratch_shapes=[pltpu.VMEM((heads, dk, dv), F32), pltpu.VMEM((heads, cb, cb), F32),
                        pltpu.VMEM((heads * dv, d), BF16), pltpu.SemaphoreType.DMA((heads,))],
        compiler_params=_cparams(1),
        name="ret_outproj",
    )(log_gamma, qkvg_first, qkvg_rest, x, w_out)


def _ple_kernel(h_ref, p_ref, pn_ref, wgate_ref, wproj_ref, nn_ref, h_out_ref, xn_out_ref, *,
                sub):
    for r in range(h_ref.shape[0] // sub):
        rows = slice(r * sub, (r + 1) * sub)
        h1 = h_ref[rows, :]
        half_emb = 0.5 * jnp.dot(p_ref[rows, :].astype(BF16), wproj_ref[...],
                                 preferred_element_type=F32)
        base = h1 + half_emb
        xn = _rms(h1, pn_ref[...]).astype(BF16)
        logits = jnp.dot(xn, wgate_ref[...], preferred_element_type=F32)
        h2 = base + half_emb * jnp.tanh(0.5 * logits)
        h_out_ref[rows, :] = h2
        xn_out_ref[rows, :] = _rms(h2, nn_ref[...]).astype(BF16)


def _ple(h, p, layer, ple_norm, w_gate, w_proj, next_norm, tm=512, sub=256):
    s, d = h.shape
    pd = p.shape[-1]
    row = lambda i: (i, 0)
    resident = lambda shape, idx: pl.BlockSpec(shape, lambda i: idx, pipeline_mode=pl.Buffered(1))
    return pl.pallas_call(
        functools.partial(_ple_kernel, sub=sub),
        grid=(s // tm,),
        in_specs=[pl.BlockSpec((tm, d), row),
                  pl.BlockSpec((None, None, tm, pd), lambda i: (layer, 0, i, 0)),
                  resident((None, 1, d), (layer, 0, 0)),
                  resident((None, d, d), (layer, 0, 0)),
                  resident((None, pd, d), (layer, 0, 0)),
                  resident((1, d), (0, 0))],
        out_specs=(pl.BlockSpec((tm, d), row), pl.BlockSpec((tm, d), row)),
        out_shape=(jax.ShapeDtypeStruct((s, d), F32), jax.ShapeDtypeStruct((s, d), BF16)),
        compiler_params=_cparams(1),
        name="ple",
    )(h, p, ple_norm, w_gate, w_proj, next_norm)


def _outproj_ple_final_kernel(z_ref, h_ref, p_ref, wout_hbm, pn_ref, wgate_hbm, wproj_hbm,
                              fn_ref, o_ref, wout_ref, wgate_ref, wproj_ref, sem, *,
                              layer, k_chunks):
    i = pl.program_id(0)
    kc = wout_ref.shape[0] // k_chunks

    def weight_copies():
        cps = [pltpu.make_async_copy(wout_hbm.at[pl.ds(k * kc, kc), :],
                                     wout_ref.at[pl.ds(k * kc, kc), :], sem.at[k])
               for k in range(k_chunks)]
        cps.append(pltpu.make_async_copy(wproj_hbm.at[layer], wproj_ref, sem.at[k_chunks]))
        cps.append(pltpu.make_async_copy(wgate_hbm.at[layer], wgate_ref, sem.at[k_chunks + 1]))
        return cps

    def body(first):
        cps = weight_copies() if first else None
        if first:
            h1 = h_ref[...]
            for k in range(k_chunks):
                cps[k].wait()
                h1 = h1 + jnp.dot(z_ref[:, k * kc:(k + 1) * kc], wout_ref[k * kc:(k + 1) * kc, :],
                                  preferred_element_type=F32)
            cps[k_chunks].wait()
        else:
            h1 = h_ref[...] + jnp.dot(z_ref[...], wout_ref[...], preferred_element_type=F32)
        half_emb = 0.5 * jnp.dot(p_ref[...].astype(BF16), wproj_ref[...],
                                 preferred_element_type=F32)
        base = h1 + half_emb
        xn = _rms(h1, pn_ref[...]).astype(BF16)
        if first:
            cps[k_chunks + 1].wait()
        logits = jnp.dot(xn, wgate_ref[...], preferred_element_type=F32)
        o_ref[...] = _rms(base + half_emb * jnp.tanh(0.5 * logits), fn_ref[...])

    @pl.when(i == 0)
    def _():
        for cp in weight_copies():
            cp.start()
        body(first=True)

    @pl.when(i > 0)
    def _():
        body(first=False)


def _outproj_ple_final(z, h, p, layer, w_out, ple_norm, w_gate, w_proj, final_norm, tm=256,
                       k_chunks=4):
    s, d = h.shape
    kz = z.shape[1]
    pd = p.shape[-1]
    row = lambda i: (i, 0)
    resident = lambda shape, idx: pl.BlockSpec(shape, lambda i: idx, pipeline_mode=pl.Buffered(1))
    hbm = pl.BlockSpec(memory_space=pl.ANY)
    return pl.pallas_call(
        functools.partial(_outproj_ple_final_kernel, layer=layer, k_chunks=k_chunks),
        grid=(s // tm,),
        in_specs=[pl.BlockSpec((tm, kz), row),
                  pl.BlockSpec((tm, d), row),
                  pl.BlockSpec((None, None, tm, pd), lambda i: (layer, 0, i, 0)),
                  hbm,
                  resident((None, 1, d), (layer, 0, 0)),
                  hbm,
                  hbm,
                  resident((1, d), (0, 0))],
        out_specs=pl.BlockSpec((None, tm, d), lambda i: (0, i, 0)),
        out_shape=jax.ShapeDtypeStruct((1, s, d), F32),
        scratch_shapes=[pltpu.VMEM((kz, d), BF16), pltpu.VMEM((d, d), BF16),
                        pltpu.VMEM((pd, d), BF16), pltpu.SemaphoreType.DMA((k_chunks + 2,))],
        compiler_params=_cparams(1),
        name="outproj_ple_final",
    )(z, h, p, w_out, ple_norm, w_gate, w_proj, final_norm)


def _pool_wfuse_kernel(wu_ref, wgrp_ref, scale_ref, o_ref):
    fused = jnp.dot(wu_ref[...].astype(BF16), wgrp_ref[...].astype(BF16),
                    preferred_element_type=F32)
    o_ref[...] = (fused * scale_ref[...]).astype(o_ref.dtype)


def _pool_wfuse(w_in, w_group, scale):
    _, d, _ = w_in.shape
    _, n_groups, gd, _ = w_group.shape
    return pl.pallas_call(
        _pool_wfuse_kernel,
        grid=(n_groups,),
        in_specs=[pl.BlockSpec((None, d, gd), lambda g: (0, 0, g)),
                  pl.BlockSpec((None, None, gd, gd), lambda g: (0, g, 0, 0)),
                  pl.BlockSpec((1, gd), lambda g: (0, g))],
        out_specs=pl.BlockSpec((d, gd), lambda g: (0, g)),
        out_shape=jax.ShapeDtypeStruct((d, n_groups * gd), BF16),
        compiler_params=_cparams(1),
        name="pool_wfuse",
    )(w_in, w_group, scale)


def _pool_kernel(xn_ref, wf_hbm, wg_hbm, o_ref, xbuf_ref, s2_ref, s4_ref, s8_ref,
                 wf_ref, wg_ref, sem, *, windows):
    i = pl.program_id(0)
    tm = xn_ref.shape[0]
    gd = o_ref.shape[1] // len(windows)
    hl = POOL_HALO
    end = hl + tm

    def weight_copies(g):
        cols = pl.ds(g * gd, gd)
        return (pltpu.make_async_copy(wg_hbm.at[:, cols], wg_ref.at[:, cols], sem.at[0, g]),
                pltpu.make_async_copy(wf_hbm.at[:, cols], wf_ref.at[:, cols], sem.at[1, g]))

    def body(first):
        xn = xn_ref[...]
        x32 = xn.astype(F32)
        xbuf_ref[hl:end, :] = x32
        s2_ref[8:end, :] = xbuf_ref[8:end, :] + xbuf_ref[7:end - 1, :]
        s4_ref[16:end, :] = s2_ref[16:end, :] + s2_ref[14:end - 2, :]
        s8_ref[24:end, :] = s4_ref[24:end, :] + s4_ref[20:end - 4, :]
        wsums = {2: s2_ref[hl:end, :], 4: s4_ref[hl:end, :], 8: s8_ref[hl:end, :],
                 16: s8_ref[hl:end, :] + s8_ref[hl - 8:end - 8, :]}
        xbuf_ref[0:hl, :] = xbuf_ref[tm:end, :]

        tok = i * tm + lax.broadcasted_iota(jnp.int32, (tm, 1), 0)
        for g, w in enumerate(windows):
            cols = slice(g * gd, (g + 1) * gd)
            if first:
                weight_copies(g)[0].wait()
            gate = jnp.dot(xn, wg_ref[:, cols], preferred_element_type=F32)
            gate = gate * _sigmoid(gate)
            inv_cnt = 1.0 / jnp.minimum(tok + 1, w).astype(F32)
            dev = (wsums[w] * inv_cnt - x32).astype(BF16)
            if first:
                weight_copies(g)[1].wait()
            mixed = jnp.dot(dev, wf_ref[:, cols], preferred_element_type=F32)
            o_ref[:, cols] = (mixed * gate).astype(o_ref.dtype)

    @pl.when(i == 0)
    def _():
        for g in range(len(windows)):
            for cp in weight_copies(g):
                cp.start()
        xbuf_ref[0:hl, :] = jnp.zeros((hl, xbuf_ref.shape[1]), F32)
        body(first=True)

    @pl.when(i > 0)
    def _():
        body(first=False)


def _pool_mixer(xn, w_fused, w_gate, tm=256):
    s, d = xn.shape
    width = w_fused.shape[1]
    assert set(POOL_WINDOWS) == {2, 4, 8, 16} and POOL_HALO >= 2 * max(POOL_WINDOWS)
    halo_buf = pltpu.VMEM((POOL_HALO + tm, d), F32)
    return pl.pallas_call(
        functools.partial(_pool_kernel, windows=POOL_WINDOWS),
        grid=(s // tm,),
        in_specs=[pl.BlockSpec((tm, d), lambda i: (i, 0)),
                  pl.BlockSpec(memory_space=pl.ANY),
                  pl.BlockSpec(memory_space=pl.ANY)],
        out_specs=pl.BlockSpec((tm, width), lambda i: (i, 0)),
        out_shape=jax.ShapeDtypeStruct((s, width), BF16),
        scratch_shapes=[halo_buf, halo_buf, halo_buf, halo_buf,
                        pltpu.VMEM((d, width), BF16), pltpu.VMEM((d, width), BF16),
                        pltpu.SemaphoreType.DMA((2, len(POOL_WINDOWS)))],
        compiler_params=_cparams(1, POOL_VMEM_LIMIT),
        name="pool_mixer",
    )(xn, w_fused, w_gate)


def _rotary_tables(seq, dk, tm):
    freq = ROPE_BASE ** (-jnp.linspace(0.0, 1.0, dk // 2, dtype=F32))
    ang_r = jnp.arange(tm, dtype=F32)[:, None] * freq[None, :]
    ang_t = (jnp.arange(seq // tm, dtype=F32) * tm)[:, None, None] * freq[None, None, :]
    return jnp.cos(ang_r), jnp.sin(ang_r), jnp.cos(ang_t), jnp.sin(ang_t)


def kernel(x, p, ret_norm, ret_w_in, ret_w_out, pool_norm, pool_w_in, pool_w_group, pool_scale,
           pool_w_out, ple_norm, ple_w_gate, ple_w_proj, final_norm):
    b, s, d = x.shape
    assert b == 1
    heads = RET_HEADS
    dk = d // heads
    dv = ret_w_out.shape[1] // heads
    assert ret_w_in.shape[2] == 2 * heads * dk + 2 * heads * dv
    tm_inproj = tn_inproj = 1024

    tabs = _rotary_tables(s, dk, tm_inproj)
    log_gamma = jnp.log1p(-(2.0 ** (-5.0 - jnp.arange(heads, dtype=F32))))
    ple_norm3 = ple_norm.reshape(ple_norm.shape[0], 1, d)
    n_layers, pd, _ = ple_w_proj.shape
    n_groups, gd = pool_w_group.shape[1:3]
    pool_width = n_groups * gd

    whole = (0, 1)
    side = [(ret_w_out.reshape(heads * dv, d), 64, whole),
            (pool_w_in.reshape(d, 2 * pool_width), 32, (1, 2)),
            (pool_w_out.reshape(pool_width, d), 64, whole),
            (ple_w_gate.reshape(n_layers * d, d), 64, whole),
            (ple_w_proj.reshape(n_layers * pd, d), 32, whole)]

    n_tiles = ret_w_in.shape[2] // tn_inproj
    widths = (heads, heads * dk, heads * dv, dk ** -0.5, tm_inproj)
    qkvg_first, (xn0,) = _ret_inproj(x, ret_w_in, tabs, log_gamma, [], *widths, 0, 1, gain=ret_norm,
                                     n_wchunks=1)
    qkvg_rest, (w_ret_out, w_pool_gate, w_pool_out, w_gate, w_proj) = _ret_inproj(
        xn0, ret_w_in, tabs, log_gamma, side, *widths, 1, n_tiles - 1)
    w_gate = w_gate.reshape(n_layers, d, d)
    w_proj = w_proj.reshape(n_layers, pd, d)
    h1 = _ret_outproj(qkvg_first, qkvg_rest, x, w_ret_out, log_gamma, heads, dk, dv)
    h2, xn2 = _ple(h1, p, 0, ple_norm3, w_gate, w_proj, pool_norm)

    w_pool_fused = _pool_wfuse(pool_w_in, pool_w_group, pool_scale)
    z = _pool_mixer(xn2, w_pool_fused, w_pool_gate)
    return _outproj_ple_final(z, h2, p, 1, w_pool_out, ple_norm3, w_gate, w_proj,
                              final_norm.reshape(1, d))
```

```python
import functools

import jax
import jax.numpy as jnp
from jax import lax
from jax.experimental import pallas as pl
from jax.experimental.pallas import tpu as pltpu

F32 = jnp.float32
BF16 = jnp.bfloat16

RET_HEADS = 8
ROPE_BASE = 10000.0
POOL_WINDOWS = (2, 4, 8, 16)
NORM_EPS = 1e-6
GN_EPS = 1e-5

LANES = 128
MXU_WIDTH = 256
POOL_HALO = 32
RET_BLOCK = 256
VMEM_LIMIT = 56 * 1024 * 1024
POOL_VMEM_LIMIT = 60 * 1024 * 1024


def _cparams(n_axes, vmem_limit=VMEM_LIMIT):
    return pltpu.CompilerParams(
        dimension_semantics=("arbitrary",) * n_axes,
        vmem_limit_bytes=vmem_limit,
    )


def _rms(x, gain):
    ms = jnp.mean(x * x, axis=-1, keepdims=True)
    return x * lax.rsqrt(ms + NORM_EPS) * gain


def _sigmoid(x):
    return 0.5 * jnp.tanh(0.5 * x) + 0.5


def _silu(x):
    h = 0.5 * x
    return h * jnp.tanh(h) + h


def _ret_inproj_kernel(lg_ref, lhs_ref, *rest, fuse_norm, n_wchunks, n_side, side_mults, tile0, n_tiles,
                       heads, n_q_tiles, n_rot_tiles, n_plain_tiles, k_scale, cb):
    if fuse_norm:
        gain_ref, rest = rest[0], rest[1:]
    w_refs, rest = rest[:n_wchunks], rest[n_wchunks:]
    cr_ref, sr_ref, ct_ref, st_ref = rest[:4]
    rest = rest[4:]
    side_in, rest = rest[:n_side], rest[n_side:]
    o_ref, rest = rest[0], rest[1:]
    if fuse_norm:
        xn_ref, rest = rest[0], rest[1:]
    else:
        xn_ref = lhs_ref
    side_out, rest = rest[:n_side], rest[n_side:]
    wbf_ref, cs_ref, sn_ref = rest
    n = tile0 if n_tiles == 1 else pl.program_id(0) + tile0
    i = pl.program_id(1)
    tm, tn = o_ref.shape
    heads_per_tile = tn // MXU_WIDTH
    wrows = wbf_ref.shape[0] // n_wchunks

    def when(cond):
        if isinstance(cond, bool):
            return lambda f: f() if cond else None
        return pl.when(cond)

    @when(n >= n_rot_tiles)
    def _():
        @pl.when(i == 0)
        def _():
            for k, w_ref in enumerate(w_refs):
                wbf_ref[k * wrows:(k + 1) * wrows, :] = w_ref[...].astype(BF16)

    @when(n < n_rot_tiles)
    def _():
        @pl.when(i == 0)
        def _():
            src = lax.broadcasted_iota(jnp.int32, (MXU_WIDTH, MXU_WIDTH), 0)
            dst = lax.broadcasted_iota(jnp.int32, (MXU_WIDTH, MXU_WIDTH), 1)
            half = MXU_WIDTH // 2
            wanted = jnp.where(dst < half, 2 * dst, 2 * (dst - half) + 1)
            perm = (src == wanted).astype(BF16)
            for k, w_ref in enumerate(w_refs):
                for c in range(heads_per_tile):
                    cols = slice(c * MXU_WIDTH, (c + 1) * MXU_WIDTH)
                    wbf_ref[k * wrows:(k + 1) * wrows, cols] = jnp.dot(
                        w_ref[:, cols].astype(BF16), perm,
                        preferred_element_type=F32).astype(BF16)

    if fuse_norm:
        xn_ref[...] = _rms(lhs_ref[...], gain_ref[...]).astype(xn_ref.dtype)

    def run(epilogue):
        for src, dst, mult in zip(side_in, side_out, side_mults):
            dst[...] = (src[...] if mult == 1.0 else src[...] * mult).astype(BF16)
        for c in range(heads_per_tile):
            cols = slice(c * MXU_WIDTH, (c + 1) * MXU_WIDTH)
            acc = jnp.dot(xn_ref[...], wbf_ref[:, cols], preferred_element_type=F32)
            o_ref[:, cols] = epilogue(acc, c).astype(o_ref.dtype)

    @when(n < n_rot_tiles)
    def _():
        ct, st = ct_ref[...], st_ref[...]
        cr, sr = cr_ref[...], sr_ref[...]
        cs_ref[...] = ct * cr - st * sr
        sn_ref[...] = st * cr + ct * sr
        is_k = n >= n_q_tiles
        row = lax.broadcasted_iota(jnp.int32, (tm, LANES), 0)
        j = (row & (cb - 1)).astype(F32)
        expo = jnp.where(is_k, cb - 1.0 - j, j + 1.0)
        scale = jnp.where(is_k, k_scale, 1.0).astype(F32)

        def rotate(acc, c):
            head = (n * heads_per_tile + c) % heads
            decay = jnp.exp(expo * lg_ref[head]) * scale
            cd = cs_ref[...] * decay
            sd = sn_ref[...] * decay
            xe, xo = acc[:, :LANES], acc[:, LANES:]
            return jnp.concatenate([xe * cd - xo * sd, xo * cd + xe * sd], axis=1)

        run(rotate)

    @when((n >= n_rot_tiles) & (n < n_plain_tiles))
    def _():
        run(lambda acc, c: acc)

    @when(n >= n_plain_tiles)
    def _():
        run(lambda acc, c: _silu(acc))


def _ret_inproj(lhs, w_in, tabs, log_gamma, side_weights, heads, qk_width, v_width, k_scale, tm,
                tile0, n_tiles, gain=None, tn=1024, cb=RET_BLOCK, n_wchunks=2):
    fuse_norm = gain is not None
    assert not (fuse_norm and (side_weights or n_tiles != 1))
    s, d = lhs.shape[-2:]
    cr, sr, ct, st = tabs
    head_w = cr.shape[1]
    assert 2 * head_w == MXU_WIDTH == 2 * LANES and cr.shape[0] == tm
    assert tm % cb == 0 and cb & (cb - 1) == 0
    n_i = s // tm
    n_rot_tiles = 2 * qk_width // tn
    kern = functools.partial(_ret_inproj_kernel, fuse_norm=fuse_norm, n_wchunks=n_wchunks,
                             n_side=len(side_weights),
                             side_mults=tuple(sw[3] for sw in side_weights),
                             tile0=tile0, n_tiles=n_tiles, heads=heads, n_q_tiles=qk_width // tn,
                             n_rot_tiles=n_rot_tiles, n_plain_tiles=n_rot_tiles + v_width // tn,
                             k_scale=k_scale, cb=cb)
    side_in_specs, side_specs, side_shapes = [], [], []
    for arr, chunk_rows, (col_blk, n_col_blks), _ in side_weights:
        rows, cols = arr.shape[0], arr.shape[1] // n_col_blks
        n_chunks = rows // chunk_rows
        assert n_chunks * chunk_rows == rows and n_chunks <= n_tiles * n_i
        in_map = lambda n, i, last=n_chunks - 1, cb_=col_blk: (jnp.minimum(n * n_i + i, last), cb_)
        out_map = lambda n, i, last=n_chunks - 1: (jnp.minimum(n * n_i + i, last), 0)
        side_in_specs.append(pl.BlockSpec((chunk_rows, cols), in_map))
        side_specs.append(pl.BlockSpec((chunk_rows, cols), out_map))
        side_shapes.append(jax.ShapeDtypeStruct((rows, cols), BF16))
    if fuse_norm:
        lhs_specs = [pl.BlockSpec((None, tm, d), lambda n, i: (0, i, 0)),
                     pl.BlockSpec((1, d), lambda n, i: (0, 0))]
        lhs_args = (lhs, gain)
        extra_specs = [pl.BlockSpec((tm, d), lambda n, i: (i, 0))]
        extra_shapes = [jax.ShapeDtypeStruct((s, d), BF16)]
        w_mode = dict(pipeline_mode=pl.Buffered(1))
    else:
        lhs_specs = [pl.BlockSpec((tm, d), lambda n, i: (i, 0))]
        lhs_args = (lhs,)
        extra_specs, extra_shapes, w_mode = [], [], {}
    last_tile = tile0 + n_tiles - 1
    stride = n_i // n_wchunks
    assert stride * n_wchunks == n_i and d % n_wchunks == 0

    def w_map(n, i, k):
        ahead = jnp.where(i > k * stride, 1, 0)
        return (0, k, jnp.minimum(n + tile0 + ahead, last_tile))

    w_specs = [pl.BlockSpec((None, d // n_wchunks, tn), functools.partial(w_map, k=k), **w_mode)
               for k in range(n_wchunks)]
    const = lambda n, i: (0, 0)
    outs = pl.pallas_call(
        kern,
        grid=(n_tiles, n_i),
        in_specs=[pl.BlockSpec(memory_space=pltpu.SMEM)] + lhs_specs + w_specs + [
            pl.BlockSpec((tm, head_w), const),
            pl.BlockSpec((tm, head_w), const),
            pl.BlockSpec((None, 1, head_w), lambda n, i: (i, 0, 0)),
            pl.BlockSpec((None, 1, head_w), lambda n, i: (i, 0, 0))] + side_in_specs,
        out_specs=[pl.BlockSpec((tm, tn), lambda n, i: (i, n))] + extra_specs + side_specs,
        out_shape=[jax.ShapeDtypeStruct((s, n_tiles * tn), BF16)] + extra_shapes + side_shapes,
        scratch_shapes=[pltpu.VMEM((d, tn), BF16),
                        pltpu.VMEM((tm, head_w), F32),
                        pltpu.VMEM((tm, head_w), F32)],
        compiler_params=_cparams(2),
        name="ret_inproj_first" if fuse_norm else "ret_inproj",
    )(log_gamma, *lhs_args, *([w_in] * n_wchunks), cr, sr, ct, st,
      *[sw[0] for sw in side_weights])
    return outs[0], outs[1:]


def _ret_outproj_kernel(lg_ref, first_ref, rest_ref, x_ref, wout_hbm, o_ref, state_ref, dmat_ref,
                        wout_ref, sem, *, heads, dk, dv):
    t = pl.program_id(0)
    cb, n_first = first_ref.shape

    def weight_copies():
        return [pltpu.make_async_copy(wout_hbm.at[pl.ds(h * dv, dv), :],
                                      wout_ref.at[pl.ds(h * dv, dv), :], sem.at[h])
                for h in range(heads)]

    def cols(start, width):
        if start + width <= n_first:
            return first_ref[:, start:start + width]
        assert start >= n_first
        return rest_ref[:, start - n_first:start - n_first + width]

    @pl.when(t == 0)
    def _():
        for cp in weight_copies():
            cp.start()
        state_ref[...] = jnp.zeros_like(state_ref)
        r = lax.broadcasted_iota(jnp.int32, (cb, cb), 0)
        c = lax.broadcasted_iota(jnp.int32, (cb, cb), 1)
        for h in range(heads):
            undo = jnp.exp(jnp.full((cb, cb), -cb, F32) * lg_ref[h])
            dmat_ref[h] = jnp.where(r >= c, undo, 0.0)

    k0, v0, g0 = heads * dk, 2 * heads * dk, 2 * heads * dk + heads * dv

    def mix(h):
        qd = cols(h * dk, dk)
        kd = cols(k0 + h * dk, dk)
        v = cols(v0 + h * dv, dv)
        block_decay = jnp.exp(jnp.full((1, 1), cb, F32) * lg_ref[h])
        scores = lax.dot_general(qd, kd, (((1,), (1,)), ((), ())), preferred_element_type=F32)
        state = state_ref[h]
        cross = jnp.dot(qd, state.astype(BF16), preferred_element_type=F32)
        update = lax.dot_general(kd, v, (((0,), (0,)), ((), ())), preferred_element_type=F32)
        p = (scores * dmat_ref[h]).astype(BF16)
        y = cross + jnp.dot(p, v, preferred_element_type=F32)
        state_ref[h] = state * block_decay + update
        return y

    def norm_gate(h, y):
        gate = cols(g0 + h * dv, dv)
        mu = jnp.mean(y, axis=-1, keepdims=True)
        yc = y - mu
        var = jnp.mean(yc * yc, axis=-1, keepdims=True)
        return (yc * lax.rsqrt(var + GN_EPS)).astype(BF16) * gate

    def project(h, yg, acc):
        return acc + jnp.dot(yg, wout_ref[h * dv:(h + 1) * dv, :], preferred_element_type=F32)

    def body(first):
        cps = weight_copies() if first else None
        acc = x_ref[...]
        ys, ygs = {}, {}
        for step in range(heads + 2):
            if 0 <= step - 2 < heads:
                if first:
                    cps[step - 2].wait()
                acc = project(step - 2, ygs.pop(step - 2), acc)
            if step < heads:
                ys[step] = mix(step)
            if 0 <= step - 1 < heads:
                ygs[step - 1] = norm_gate(step - 1, ys.pop(step - 1))
        o_ref[...] = acc

    @pl.when(t == 0)
    def _():
        body(first=True)

    @pl.when(t > 0)
    def _():
        body(first=False)


def _ret_outproj(qkvg_first, qkvg_rest, x, w_out, log_gamma, heads, dk, dv, cb=RET_BLOCK):
    s, n_first = qkvg_first.shape
    n_rest = qkvg_rest.shape[1]
    d = w_out.shape[1]
    return pl.pallas_call(
        functools.partial(_ret_outproj_kernel, heads=heads, dk=dk, dv=dv),
        grid=(s // cb,),
        in_specs=[pl.BlockSpec(memory_space=pltpu.SMEM),
                  pl.BlockSpec((cb, n_first), lambda t: (t, 0)),
                  pl.BlockSpec((cb, n_rest), lambda t: (t, 0)),
                  pl.BlockSpec((None, cb, d), lambda t: (0, t, 0)),
                  pl.BlockSpec(memory_space=pl.ANY)],
        out_specs=pl.BlockSpec((cb, d), lambda t: (t, 0)),
        out_shape=jax.ShapeDtypeStruct((s, d), F32),
        scratch_shapes=[pltpu.VMEM((heads, dk, dv), F32), pltpu.VMEM((heads, cb, cb), F32),
                        pltpu.VMEM((heads * dv, d), BF16), pltpu.SemaphoreType.DMA((heads,))],
        compiler_params=_cparams(1),
        name="ret_outproj",
    )(log_gamma, qkvg_first, qkvg_rest, x, w_out)


def _ple_kernel(h_ref, p_ref, pn_ref, wgate_ref, wproj_ref, nn_ref, h_out_ref, xn_out_ref, *,
                sub):
    for r in range(h_ref.shape[0] // sub):
        rows = slice(r * sub, (r + 1) * sub)
        h1 = h_ref[rows, :]
        half_emb = jnp.dot(p_ref[rows, :].astype(BF16), wproj_ref[...],
                           preferred_element_type=F32)
        base = h1 + half_emb
        xn = _rms(h1, pn_ref[...]).astype(BF16)
        logits = jnp.dot(xn, wgate_ref[...], preferred_element_type=F32)
        h2 = base + half_emb * jnp.tanh(0.5 * logits)
        h_out_ref[rows, :] = h2
        xn_out_ref[rows, :] = _rms(h2, nn_ref[...]).astype(BF16)


def _ple(h, p, layer, ple_norm, w_gate, w_proj, next_norm, tm=512, sub=256):
    s, d = h.shape
    pd = p.shape[-1]
    row = lambda i: (i, 0)
    resident = lambda shape, idx: pl.BlockSpec(shape, lambda i: idx, pipeline_mode=pl.Buffered(1))
    return pl.pallas_call(
        functools.partial(_ple_kernel, sub=sub),
        grid=(s // tm,),
        in_specs=[pl.BlockSpec((tm, d), row),
                  pl.BlockSpec((None, None, tm, pd), lambda i: (layer, 0, i, 0)),
                  resident((None, 1, d), (layer, 0, 0)),
                  resident((None, d, d), (layer, 0, 0)),
                  resident((None, pd, d), (layer, 0, 0)),
                  resident((1, d), (0, 0))],
        out_specs=(pl.BlockSpec((tm, d), row), pl.BlockSpec((tm, d), row)),
        out_shape=(jax.ShapeDtypeStruct((s, d), F32), jax.ShapeDtypeStruct((s, d), BF16)),
        compiler_params=_cparams(1),
        name="ple",
    )(h, p, ple_norm, w_gate, w_proj, next_norm)


def _outproj_ple_final_kernel(z_ref, h_ref, p_ref, wout_hbm, pn_ref, wgate_hbm, wproj_hbm,
                              fn_ref, o_ref, wout_ref, wgate_ref, wproj_ref, sem, *,
                              layer, k_chunks):
    i = pl.program_id(0)
    kc = wout_ref.shape[0] // k_chunks

    def weight_copies():
        cps = [pltpu.make_async_copy(wout_hbm.at[pl.ds(k * kc, kc), :],
                                     wout_ref.at[pl.ds(k * kc, kc), :], sem.at[k])
               for k in range(k_chunks)]
        cps.append(pltpu.make_async_copy(wproj_hbm.at[layer], wproj_ref, sem.at[k_chunks]))
        cps.append(pltpu.make_async_copy(wgate_hbm.at[layer], wgate_ref, sem.at[k_chunks + 1]))
        return cps

    def body(first):
        cps = weight_copies() if first else None
        if first:
            h1 = h_ref[...]
            for k in range(k_chunks):
                cps[k].wait()
                h1 = h1 + jnp.dot(z_ref[:, k * kc:(k + 1) * kc], wout_ref[k * kc:(k + 1) * kc, :],
                                  preferred_element_type=F32)
            cps[k_chunks].wait()
        else:
            h1 = h_ref[...] + jnp.dot(z_ref[...], wout_ref[...], preferred_element_type=F32)
        half_emb = jnp.dot(p_ref[...].astype(BF16), wproj_ref[...],
                           preferred_element_type=F32)
        base = h1 + half_emb
        xn = _rms(h1, pn_ref[...]).astype(BF16)
        if first:
            cps[k_chunks + 1].wait()
        logits = jnp.dot(xn, wgate_ref[...], preferred_element_type=F32)
        o_ref[...] = _rms(base + half_emb * jnp.tanh(0.5 * logits), fn_ref[...])

    @pl.when(i == 0)
    def _():
        for cp in weight_copies():
            cp.start()
        body(first=True)

    @pl.when(i > 0)
    def _():
        body(first=False)


def _outproj_ple_final(z, h, p, layer, w_out, ple_norm, w_gate, w_proj, final_norm, tm=256,
                       k_chunks=4):
    s, d = h.shape
    kz = z.shape[1]
    pd = p.shape[-1]
    row = lambda i: (i, 0)
    resident = lambda shape, idx: pl.BlockSpec(shape, lambda i: idx, pipeline_mode=pl.Buffered(1))
    hbm = pl.BlockSpec(memory_space=pl.ANY)
    return pl.pallas_call(
        functools.partial(_outproj_ple_final_kernel, layer=layer, k_chunks=k_chunks),
        grid=(s // tm,),
        in_specs=[pl.BlockSpec((tm, kz), row),
                  pl.BlockSpec((tm, d), row),
                  pl.BlockSpec((None, None, tm, pd), lambda i: (layer, 0, i, 0)),
                  hbm,
                  resident((None, 1, d), (layer, 0, 0)),
                  hbm,
                  hbm,
                  resident((1, d), (0, 0))],
        out_specs=pl.BlockSpec((None, tm, d), lambda i: (0, i, 0)),
        out_shape=jax.ShapeDtypeStruct((1, s, d), F32),
        scratch_shapes=[pltpu.VMEM((kz, d), BF16), pltpu.VMEM((d, d), BF16),
                        pltpu.VMEM((pd, d), BF16), pltpu.SemaphoreType.DMA((k_chunks + 2,))],
        compiler_params=_cparams(1),
        name="outproj_ple_final",
    )(z, h, p, w_out, ple_norm, w_gate, w_proj, final_norm)


def _pool_wfuse_kernel(wu_ref, wgrp_ref, scale_ref, o_ref):
    fused = jnp.dot(wu_ref[...].astype(BF16), wgrp_ref[...].astype(BF16),
                    preferred_element_type=F32)
    o_ref[...] = (fused * scale_ref[...]).astype(o_ref.dtype)


def _pool_wfuse(w_in, w_group, scale):
    _, d, _ = w_in.shape
    _, n_groups, gd, _ = w_group.shape
    return pl.pallas_call(
        _pool_wfuse_kernel,
        grid=(n_groups,),
        in_specs=[pl.BlockSpec((None, d, gd), lambda g: (0, 0, g)),
                  pl.BlockSpec((None, None, gd, gd), lambda g: (0, g, 0, 0)),
                  pl.BlockSpec((1, gd), lambda g: (0, g))],
        out_specs=pl.BlockSpec((d, gd), lambda g: (0, g)),
        out_shape=jax.ShapeDtypeStruct((d, n_groups * gd), BF16),
        compiler_params=_cparams(1),
        name="pool_wfuse",
    )(w_in, w_group, scale)


def _pool_kernel(xn_ref, wf_hbm, wg_hbm, o_ref, xbuf_ref, s2_ref, s4_ref, s8_ref,
                 wf_ref, wg_ref, sem, *, windows):
    i = pl.program_id(0)
    tm = xn_ref.shape[0]
    gd = o_ref.shape[1] // len(windows)
    hl = POOL_HALO
    end = hl + tm

    def weight_copies(g):
        cols = pl.ds(g * gd, gd)
        return (pltpu.make_async_copy(wg_hbm.at[:, cols], wg_ref.at[:, cols], sem.at[0, g]),
                pltpu.make_async_copy(wf_hbm.at[:, cols], wf_ref.at[:, cols], sem.at[1, g]))

    def body(first):
        xn = xn_ref[...]
        x32 = xn.astype(F32)
        xbuf_ref[hl:end, :] = x32
        s2_ref[8:end, :] = xbuf_ref[8:end, :] + xbuf_ref[7:end - 1, :]
        s4_ref[16:end, :] = s2_ref[16:end, :] + s2_ref[14:end - 2, :]
        s8_ref[24:end, :] = s4_ref[24:end, :] + s4_ref[20:end - 4, :]
        wsums = {2: s2_ref[hl:end, :], 4: s4_ref[hl:end, :], 8: s8_ref[hl:end, :],
                 16: s8_ref[hl:end, :] + s8_ref[hl - 8:end - 8, :]}
        xbuf_ref[0:hl, :] = xbuf_ref[tm:end, :]

        tok = i * tm + lax.broadcasted_iota(jnp.int32, (tm, 1), 0)
        for g, w in enumerate(windows):
            cols = slice(g * gd, (g + 1) * gd)
            if first:
                weight_copies(g)[0].wait()
            gate = jnp.dot(xn, wg_ref[:, cols], preferred_element_type=F32)
            gate = gate * _sigmoid(gate)
            inv_cnt = 1.0 / jnp.minimum(tok + 1, w).astype(F32)
            dev = (wsums[w] * inv_cnt - x32).astype(BF16)
            if first:
                weight_copies(g)[1].wait()
            mixed = jnp.dot(dev, wf_ref[:, cols], preferred_element_type=F32)
            o_ref[:, cols] = (mixed * gate).astype(o_ref.dtype)

    @pl.when(i == 0)
    def _():
        for g in range(len(windows)):
            for cp in weight_copies(g):
                cp.start()
        xbuf_ref[0:hl, :] = jnp.zeros((hl, xbuf_ref.shape[1]), F32)
        body(first=True)

    @pl.when(i > 0)
    def _():
        body(first=False)


def _pool_mixer(xn, w_fused, w_gate, tm=256):
    s, d = xn.shape
    width = w_fused.shape[1]
    assert set(POOL_WINDOWS) == {2, 4, 8, 16} and POOL_HALO >= 2 * max(POOL_WINDOWS)
    halo_buf = pltpu.VMEM((POOL_HALO + tm, d), F32)
    return pl.pallas_call(
        functools.partial(_pool_kernel, windows=POOL_WINDOWS),
        grid=(s // tm,),
        in_specs=[pl.BlockSpec((tm, d), lambda i: (i, 0)),
                  pl.BlockSpec(memory_space=pl.ANY),
                  pl.BlockSpec(memory_space=pl.ANY)],
        out_specs=pl.BlockSpec((tm, width), lambda i: (i, 0)),
        out_shape=jax.ShapeDtypeStruct((s, width), BF16),
        scratch_shapes=[halo_buf, halo_buf, halo_buf, halo_buf,
                        pltpu.VMEM((d, width), BF16), pltpu.VMEM((d, width), BF16),
                        pltpu.SemaphoreType.DMA((2, len(POOL_WINDOWS)))],
        compiler_params=_cparams(1, POOL_VMEM_LIMIT),
        name="pool_mixer",
    )(xn, w_fused, w_gate)


def _rotary_tables(seq, dk, tm):
    freq = ROPE_BASE ** (-jnp.linspace(0.0, 1.0, dk // 2, dtype=F32))
    ang_r = jnp.arange(tm, dtype=F32)[:, None] * freq[None, :]
    ang_t = (jnp.arange(seq // tm, dtype=F32) * tm)[:, None, None] * freq[None, None, :]
    return jnp.cos(ang_r), jnp.sin(ang_r), jnp.cos(ang_t), jnp.sin(ang_t)


def kernel(x, p, ret_norm, ret_w_in, ret_w_out, pool_norm, pool_w_in, pool_w_group, pool_scale,
           pool_w_out, ple_norm, ple_w_gate, ple_w_proj, final_norm):
    b, s, d = x.shape
    assert b == 1
    heads = RET_HEADS
    dk = d // heads
    dv = ret_w_out.shape[1] // heads
    assert ret_w_in.shape[2] == 2 * heads * dk + 2 * heads * dv
    tm_inproj = tn_inproj = 1024

    tabs = _rotary_tables(s, dk, tm_inproj)
    log_gamma = jnp.log1p(-(2.0 ** (-5.0 - jnp.arange(heads, dtype=F32))))
    ple_norm3 = ple_norm.reshape(ple_norm.shape[0], 1, d)
    n_layers, pd, _ = ple_w_proj.shape
    n_groups, gd = pool_w_group.shape[1:3]
    pool_width = n_groups * gd

    whole = (0, 1)
    side = [(ret_w_out.reshape(heads * dv, d), 64, whole, 1.0),
            (pool_w_in.reshape(d, 2 * pool_width), 32, (1, 2), 1.0),
            (pool_w_out.reshape(pool_width, d), 64, whole, 1.0),
            (ple_w_gate.reshape(n_layers * d, d), 64, whole, 1.0),
            (ple_w_proj.reshape(n_layers * pd, d), 32, whole, 0.5)]

    n_tiles = ret_w_in.shape[2] // tn_inproj
    widths = (heads, heads * dk, heads * dv, dk ** -0.5, tm_inproj)
    qkvg_first, (xn0,) = _ret_inproj(x, ret_w_in, tabs, log_gamma, [], *widths, 0, 1, gain=ret_norm,
                                     n_wchunks=1)
    qkvg_rest, (w_ret_out, w_pool_gate, w_pool_out, w_gate, w_proj) = _ret_inproj(
        xn0, ret_w_in, tabs, log_gamma, side, *widths, 1, n_tiles - 1)
    w_gate = w_gate.reshape(n_layers, d, d)
    w_proj = w_proj.reshape(n_layers, pd, d)
    h1 = _ret_outproj(qkvg_first, qkvg_rest, x, w_ret_out, log_gamma, heads, dk, dv)
    h2, xn2 = _ple(h1, p, 0, ple_norm3, w_gate, w_proj, pool_norm)

    w_pool_fused = _pool_wfuse(pool_w_in, pool_w_group, pool_scale)
    z = _pool_mixer(xn2, w_pool_fused, w_pool_gate)
    return _outproj_ple_final(z, h2, p, 1, w_pool_out, ple_norm3, w_gate, w_proj,
                              final_norm.reshape(1, d))
```
